```python
import math
import jax, jax.numpy as jnp
from jax import lax
import numpy as np

D_MODEL = 1024
BATCH = 2
SEQ = 8192
DEPTH = 2
DEC_BATCH = 32
DEC_SEQ = 1
PAST_LEN = 16384
PAGE_SIZE = 128

N_META = 16
N_EVEN = (DEPTH + 1) // 2
N_ODD = DEPTH // 2
POOL_WIDTH = D_MODEL // 2
POOL_WINDOWS = (2, 4, 8, 16)
N_POOL_GROUPS = len(POOL_WINDOWS)
POOL_GROUP = POOL_WIDTH // N_POOL_GROUPS
POOL_HIST = max(POOL_WINDOWS) - 1
SB_HEADS = 8
SB_HEAD_DIM = (D_MODEL // 2) // SB_HEADS
SB_WIDTH = SB_HEADS * SB_HEAD_DIM
SB_BIAS_INIT = -6.0
Q_BLOCK = 128
RET_HEADS = 4
RET_QK_DIM = D_MODEL // RET_HEADS
RET_V_DIM = 2 * D_MODEL // RET_HEADS
RET_CHUNK = 128
ROPE_BASE = 10000.0
D_FF = 2816
EPS = 1e-6
F32 = jnp.float32

kernel_name = 'hybrid_pool_stickbreak_retnet_step'


def rmsnorm(x, g):
    xf = x.astype(F32)
    y = xf * lax.rsqrt(jnp.mean(xf * xf, axis=-1, keepdims=True) + EPS)
    return (y * g.astype(F32)).astype(x.dtype)


def head_norm(o):
    return o * lax.rsqrt(jnp.mean(o * o, axis=-1, keepdims=True) + EPS)


def ffn_half(x, g_pre, g_post, w_gate, w_up, w_down):
    h = rmsnorm(x, g_pre)
    f = (jax.nn.silu(h @ w_gate) * (h @ w_up)) @ w_down
    return x + 0.5 * rmsnorm(f, g_post)


def pool_mix(hist, v, pos0, w_pool, pool_scale):
    B, L, _ = v.shape
    vf = v.astype(F32)
    full = jnp.concatenate([jnp.zeros((B, 1, POOL_WIDTH), F32), hist.astype(F32), vf], axis=1)
    cs = jnp.cumsum(full, axis=1)
    end = cs[:, POOL_HIST + 1:]
    pos = pos0 + jnp.arange(L)
    outs = []
    for gi, w in enumerate(POOL_WINDOWS):
        sl = slice(gi * POOL_GROUP, (gi + 1) * POOL_GROUP)
        start = cs[:, POOL_HIST + 1 - w: POOL_HIST + 1 - w + L, sl]
        cnt = jnp.minimum(w, pos + 1).astype(F32)[None, :, None]
        outs.append((end[..., sl] - start) / cnt - vf[..., sl])
    pooled = jnp.stack(outs, axis=2)
    mixed = jnp.einsum('blgc,gcd->blgd', pooled, w_pool.astype(F32))
    return (mixed.reshape(B, L, POOL_WIDTH) * pool_scale.astype(F32)).astype(v.dtype)


def sb_attend(q, k, v, q_pos, k_pos, sb_bias):
    z = jnp.einsum('bqhd,bkhd->bhqk', q.astype(F32), k.astype(F32)) * (SB_HEAD_DIM ** -0.5) \
        + sb_bias.astype(F32)[None, :, None, None]
    causal = (k_pos[None, :] < q_pos[:, None])[None, None]
    log_1m = jnp.where(causal, jax.nn.log_sigmoid(-z), 0.0)
    after = lax.cumsum(log_1m, axis=3, reverse=True) - log_1m
    w = jnp.where(causal, jnp.exp(jax.nn.log_sigmoid(z) + after), 0.0)
    return jnp.einsum('bhqk,bkhd->bqhd', w, v.astype(F32)).astype(q.dtype)


def stick_breaking_prompt(q, k, v, sb_bias):
    B, T, H, Dh = q.shape
    n_blk = -(-T // Q_BLOCK)
    Tp = n_blk * Q_BLOCK
    pad = ((0, 0), (0, Tp - T), (0, 0), (0, 0))
    qp, kp, vp = jnp.pad(q, pad), jnp.pad(k, pad), jnp.pad(v, pad)
    k_pos = jnp.arange(Tp)

    def block(i):
        qb = lax.dynamic_slice_in_dim(qp, i * Q_BLOCK, Q_BLOCK, axis=1)
        q_pos = i * Q_BLOCK + jnp.arange(Q_BLOCK)
        return sb_attend(qb, kp, vp, q_pos, k_pos, sb_bias)

    out = lax.map(block, jnp.arange(n_blk))
    return jnp.moveaxis(out, 0, 1).reshape(B, Tp, H, Dh)[:, :T]


def split_even(hp):
    B, L, _ = hp.shape
    pool_in = hp[..., :POOL_WIDTH]
    q, k, v = jnp.split(hp[..., POOL_WIDTH:], 3, axis=-1)
    shp = (B, L, SB_HEADS, SB_HEAD_DIM)
    return pool_in, q.reshape(shp), k.reshape(shp), v.reshape(shp)


def even_mixer_prompt(h, w_in, w_out, w_pool, pool_scale, sb_bias):
    B, T, _ = h.shape
    pool_in, q, k, v = split_even(h @ w_in)
    hist0 = jnp.zeros((B, POOL_HIST, POOL_WIDTH), pool_in.dtype)
    pool_out = pool_mix(hist0, pool_in, 0, w_pool, pool_scale)
    sb_out = stick_breaking_prompt(q, k, v, sb_bias).reshape(B, T, SB_WIDTH)
    out = jnp.concatenate([pool_out, sb_out], axis=-1) @ w_out
    return out, pool_in[:, -POOL_HIST:], k, v


def even_mixer_sample(h, hist, k_past, v_past, w_in, w_out, w_pool, pool_scale, sb_bias):
    B, L, _ = h.shape
    pool_in, q, k, v = split_even(h @ w_in)
    pool_out = pool_mix(hist, pool_in, PAST_LEN, w_pool, pool_scale)
    k_all = jnp.concatenate([k_past.astype(k.dtype), k], axis=1)
    v_all = jnp.concatenate([v_past.astype(v.dtype), v], axis=1)
    q_pos = PAST_LEN + jnp.arange(L)
    k_pos = jnp.arange(k_all.shape[1])
    sb_out = sb_attend(q, k_all, v_all, q_pos, k_pos, sb_bias).reshape(B, L, SB_WIDTH)
    out = jnp.concatenate([pool_out, sb_out], axis=-1) @ w_out
    new_hist = jnp.concatenate([hist.astype(pool_in.dtype), pool_in], axis=1)[:, -POOL_HIST:]
    return out, new_hist, k, v


def rotary(x, pos):
    half = x.shape[-1] // 2
    inv = ROPE_BASE ** (-jnp.arange(half, dtype=F32) / half)
    ang = pos.astype(F32)[:, None] * inv[None, :]
    cos = jnp.cos(ang)[None, :, None, :]
    sin = jnp.sin(ang)[None, :, None, :]
    x1, x2 = x[..., :half], x[..., half:]
    return jnp.concatenate([x1 * cos - x2 * sin, x1 * sin + x2 * cos], axis=-1)


def log_gammas():
    return jnp.log1p(-jnp.exp2(-5.0 - jnp.arange(RET_HEADS, dtype=F32)))


def retention_chunk(state, q, k, v, lg):
    L = q.shape[1]
    idx = jnp.arange(L, dtype=F32)
    diff = idx[:, None] - idx[None, :]
    decay = jnp.where(diff[None] >= 0, jnp.exp(lg[:, None, None] * jnp.maximum(diff, 0.0)[None]), 0.0)
    scores = jnp.einsum('blhd,bmhd->bhlm', q, k) * decay[None]
    inner = jnp.einsum('bhlm,bmhe->blhe', scores, v)
    q_dec = jnp.exp(lg[None, :] * (idx[:, None] + 1.0))
    cross = jnp.einsum('blhd,bhde->blhe', q * q_dec[None, :, :, None], state)
    k_dec = jnp.exp(lg[None, :] * (L - 1.0 - idx[:, None]))
    new_state = state * jnp.exp(lg * L)[None, :, None, None] + jnp.einsum(
        'blhd,blhe->bhde', k * k_dec[None, :, :, None], v)
    return new_state, inner + cross


def retention_prompt(q, k, v, lg):
    B, T = q.shape[:2]
    state0 = jnp.zeros((B, RET_HEADS, RET_QK_DIM, RET_V_DIM), F32)
    state, o_meta = retention_chunk(state0, q[:, :N_META], k[:, :N_META], v[:, :N_META], lg)
    n_c = (T - N_META) // RET_CHUNK

    def to_chunks(a):
        return jnp.moveaxis(a[:, N_META:].reshape(B, n_c, RET_CHUNK, *a.shape[2:]), 1, 0)

    def step(s, qkv):
        return retention_chunk(s, qkv[0], qkv[1], qkv[2], lg)

    state, o = lax.scan(step, state, (to_chunks(q), to_chunks(k), to_chunks(v)))
    o = jnp.moveaxis(o, 0, 1).reshape(B, n_c * RET_CHUNK, RET_HEADS, RET_V_DIM)
    return jnp.concatenate([o_meta, o], axis=1), state


def split_odd(hp, pos):
    B, L, _ = hp.shape
    dq = RET_HEADS * RET_QK_DIM
    dv = RET_HEADS * RET_V_DIM
    q, k, v, g = jnp.split(hp, [dq, 2 * dq, 2 * dq + dv], axis=-1)
    q = rotary(q.reshape(B, L, RET_HEADS, RET_QK_DIM).astype(F32), pos)
    k = rotary(k.reshape(B, L, RET_HEADS, RET_QK_DIM).astype(F32), pos) * (RET_QK_DIM ** -0.5)
    v = v.reshape(B, L, RET_HEADS, RET_V_DIM).astype(F32)
    return q, k, v, g


def odd_output(o, g, w_out):
    B, L = o.shape[:2]
    o = head_norm(o).reshape(B, L, RET_HEADS * RET_V_DIM)
    return (jax.nn.silu(g.astype(F32)) * o).astype(g.dtype) @ w_out


def odd_mixer_prompt(h, pos, lg, w_in, w_out):
    q, k, v, g = split_odd(h @ w_in, pos)
    o, state = retention_prompt(q, k, v, lg)
    return odd_output(o, g, w_out), state


def odd_mixer_sample(h, pos, state, lg, w_in, w_out):
    q, k, v, g = split_odd(h @ w_in, pos)
    new_state, o = retention_chunk(state.astype(F32), q, k, v, lg)
    return odd_output(o, g, w_out), new_state


def setup_inputs(seed: int = 0) -> dict:
    key = jax.random.key(seed)
    ks = jax.random.split(key, 20)
    n_pages = PAST_LEN // PAGE_SIZE
    n_used = DEC_BATCH * n_pages
    n_phys = (5 * n_used + 3) // 4

    def nrm(k, shape, scale):
        return jax.random.normal(k, shape, F32) * scale

    d_in_even = POOL_WIDTH + 3 * SB_WIDTH
    d_in_odd = 2 * RET_HEADS * RET_QK_DIM + 2 * RET_HEADS * RET_V_DIM
    return {
        'x_prompt': nrm(ks[0], (BATCH, SEQ, D_MODEL), 1.0),
        'x_sample': nrm(ks[1], (DEC_BATCH, DEC_SEQ, D_MODEL), 1.0),
        'cache_k': nrm(ks[2], (N_EVEN, n_phys, PAGE_SIZE, SB_HEADS, SB_HEAD_DIM), 1.0),
        'cache_v': nrm(ks[3], (N_EVEN, n_phys, PAGE_SIZE, SB_HEADS, SB_HEAD_DIM), 1.0),
        'state_pool': nrm(ks[4], (N_EVEN, DEC_BATCH, POOL_HIST, POOL_WIDTH), 1.0),
        'state_ret': nrm(ks[5], (N_ODD, DEC_BATCH, RET_HEADS, RET_QK_DIM, RET_V_DIM), 0.1),
        'page_table': jax.random.permutation(ks[6], n_phys)[:n_used].reshape(DEC_BATCH, n_pages).astype(jnp.int32),
        'meta_tokens': nrm(ks[7], (N_META, D_MODEL), 1.0),
        'norm_g': 1.0 + nrm(ks[8], (DEPTH, 6, D_MODEL), 0.02),
        'ffn_w_gate': nrm(ks[9], (DEPTH, 2, D_MODEL, D_FF), D_MODEL ** -0.5),
        'ffn_w_up': nrm(ks[10], (DEPTH, 2, D_MODEL, D_FF), D_MODEL ** -0.5),
        'ffn_w_down': nrm(ks[11], (DEPTH, 2, D_FF, D_MODEL), D_FF ** -0.5),
        'even_w_in': nrm(ks[12], (N_EVEN, D_MODEL, d_in_even), D_MODEL ** -0.5),
        'even_w_out': nrm(ks[13], (N_EVEN, POOL_WIDTH + SB_WIDTH, D_MODEL), (POOL_WIDTH + SB_WIDTH) ** -0.5),
        'pool_w': nrm(ks[14], (N_EVEN, N_POOL_GROUPS, POOL_GROUP, POOL_GROUP), POOL_GROUP ** -0.5),
        'pool_scale': 1.0 + nrm(ks[15], (N_EVEN, POOL_WIDTH), 0.02),
        'sb_bias': SB_BIAS_INIT + nrm(ks[18], (N_EVEN, SB_HEADS), 0.1),
        'odd_w_in': nrm(ks[16], (N_ODD, D_MODEL, d_in_odd), D_MODEL ** -0.5),
        'odd_w_out': nrm(ks[17], (N_ODD, RET_HEADS * RET_V_DIM, D_MODEL), (RET_HEADS * RET_V_DIM) ** -0.5),
    }


def reference(x_prompt, x_sample, cache_k, cache_v, state_pool, state_ret, page_table,
              meta_tokens, norm_g, ffn_w_gate, ffn_w_up, ffn_w_down,
              even_w_in, even_w_out, pool_w, pool_scale, sb_bias, odd_w_in, odd_w_out):
    B = x_prompt.shape[0]
    DB = x_sample.shape[0]
    xp = jnp.concatenate([jnp.broadcast_to(meta_tokens[None].astype(x_prompt.dtype), (B, N_META, D_MODEL)),
                          x_prompt], axis=1)
    xs = x_sample
    pos_p = jnp.arange(xp.shape[1])
    pos_s = PAST_LEN + jnp.arange(xs.shape[1])
    lg = log_gammas()
    n_pages = page_table.shape[1]
    pool_p, pool_s, k_p, v_p, k_s, v_s, ret_p, ret_s = [], [], [], [], [], [], [], []
    for layer in range(DEPTH):
        g = norm_g[layer]
        xp = ffn_half(xp, g[0], g[1], ffn_w_gate[layer, 0], ffn_w_up[layer, 0], ffn_w_down[layer, 0])
        xs = ffn_half(xs, g[0], g[1], ffn_w_gate[layer, 0], ffn_w_up[layer, 0], ffn_w_down[layer, 0])
        hp = rmsnorm(xp, g[2])
        hs = rmsnorm(xs, g[2])
        if layer % 2 == 0:
            e = layer // 2
            mp, hist_p, kk_p, vv_p = even_mixer_prompt(hp, even_w_in[e], even_w_out[e], pool_w[e],
                                                       pool_scale[e], sb_bias[e])
            k_past = cache_k[e][page_table].reshape(DB, n_pages * PAGE_SIZE, SB_HEADS, SB_HEAD_DIM)
            v_past = cache_v[e][page_table].reshape(DB, n_pages * PAGE_SIZE, SB_HEADS, SB_HEAD_DIM)
            ms, hist_s, kk_s, vv_s = even_mixer_sample(hs, state_pool[e], k_past, v_past,
                                                       even_w_in[e], even_w_out[e], pool_w[e],
                                                       pool_scale[e], sb_bias[e])
            pool_p.append(hist_p)
            pool_s.append(hist_s)
            k_p.append(kk_p)
            v_p.append(vv_p)
            k_s.append(kk_s)
            v_s.append(vv_s)
        else:
            o = layer // 2
            mp, st_p = odd_mixer_prompt(hp, pos_p, lg, odd_w_in[o], odd_w_out[o])
            ms, st_s = odd_mixer_sample(hs, pos_s, state_ret[o], lg, odd_w_in[o], odd_w_out[o])
            ret_p.append(st_p.astype(x_prompt.dtype))
            ret_s.append(st_s.astype(x_sample.dtype))
        xp = xp + rmsnorm(mp, g[3])
        xs = xs + rmsnorm(ms, g[3])
        xp = ffn_half(xp, g[4], g[5], ffn_w_gate[layer, 1], ffn_w_up[layer, 1], ffn_w_down[layer, 1])
        xs = ffn_half(xs, g[4], g[5], ffn_w_gate[layer, 1], ffn_w_up[layer, 1], ffn_w_down[layer, 1])
    y_prompt = xp[:, N_META:]
    return (y_prompt, xs, jnp.stack(pool_p), jnp.stack(pool_s), jnp.stack(k_p), jnp.stack(v_p),
            jnp.stack(k_s), jnp.stack(v_s), jnp.stack(ret_p), jnp.stack(ret_s))
```

```python
import functools
import math

import jax
import jax.numpy as jnp
from jax import lax
from jax.experimental import pallas as pl
from jax.experimental.pallas import tpu as pltpu

F32 = jnp.float32
BF16 = jnp.bfloat16

D_MODEL = 1024
SEQ = 8192
PAST_LEN = 16384
PAGE_SIZE = 128
N_META = 16
POOL_WIDTH = D_MODEL // 2
POOL_WINDOWS = (2, 4, 8, 16)
POOL_GROUP = POOL_WIDTH // len(POOL_WINDOWS)
POOL_HIST = max(POOL_WINDOWS) - 1
SB_HEADS = 8
SB_HEAD_DIM = (D_MODEL // 2) // SB_HEADS
SB_WIDTH = SB_HEADS * SB_HEAD_DIM
RET_HEADS = 4
RET_QK_DIM = D_MODEL // RET_HEADS
RET_V_DIM = 2 * D_MODEL // RET_HEADS
ROPE_BASE = 10000.0
D_FF = 2816
EPS = 1e-6

BLK = 128
T_REAL = N_META + SEQ
NB = -(-T_REAL // BLK)
TP = NB * BLK
PAD = TP - T_REAL
HIST_ROWS = 16
TM = 640
FF_CHUNK = 256
VMEM_LIMIT = 56 * 1024 * 1024


def _params(n_grid):
    return pltpu.CompilerParams(dimension_semantics=("arbitrary",) * n_grid,
                                vmem_limit_bytes=VMEM_LIMIT)


def _const_spec(shape):
    nd = len(shape)
    return pl.BlockSpec(shape, lambda *_: (0,) * nd, pipeline_mode=pl.Buffered(1))


def _rms(x, g):
    return x * lax.rsqrt(jnp.mean(x * x, axis=-1, keepdims=True) + EPS) * g


def _dot(a, b):
    return jnp.dot(a, b, preferred_element_type=F32)


def _dot_nt(a, b):
    return lax.dot_general(a, b, (((1,), (1,)), ((), ())), preferred_element_type=F32)


def _dot_tn(a, b):
    return lax.dot_general(a, b, (((0,), (0,)), ((), ())), preferred_element_type=F32)


def _neg_softplus(z):
    return -(jnp.maximum(z, 0.0) + jnp.log(1.0 + jnp.exp(-jnp.abs(z))))


def _cumsum_matrix():
    r = lax.broadcasted_iota(jnp.int32, (2 * BLK, 2 * BLK), 0) & (BLK - 1)
    c = lax.broadcasted_iota(jnp.int32, (2 * BLK, 2 * BLK), 1)
    return jnp.where((c >= BLK) | (r >= c), 1.0, 0.0).astype(BF16)


def _suffix_sums(lm, carry, m2s):
    hi = lm.astype(BF16)
    lo = (lm - hi.astype(F32)).astype(BF16)
    res = _dot(jnp.concatenate([hi, lo], axis=1), m2s) + jnp.concatenate([carry, carry], axis=1)
    return res[:, :BLK], res[:, BLK:]


def _ffn_kernel(x_ref, g_ref, wg_ref, wu_ref, wd_ref, o_ref, act_ref):
    x = x_ref[...]
    h = _rms(x, g_ref[0:1, :]).astype(BF16)
    for c in range(D_FF // FF_CHUNK):
        sl = slice(c * FF_CHUNK, (c + 1) * FF_CHUNK)
        gate = _dot(h, wg_ref[:, sl])
        up = _dot(h, wu_ref[:, sl])
        act_ref[:, sl] = (gate * jax.nn.sigmoid(gate) * up).astype(BF16)
    f = _dot(act_ref[...], wd_ref[...])
    o_ref[...] = x + 0.5 * _rms(f, g_ref[1:2, :])


def _ffn(x, g2, wg, wu, wd, tm):
    n = x.shape[0]
    return pl.pallas_call(
        _ffn_kernel,
        grid=(n // tm,),
        in_specs=[pl.BlockSpec((tm, D_MODEL), lambda i: (i, 0)),
                  _const_spec((2, D_MODEL)),
                  _const_spec((D_MODEL, D_FF)),
                  _const_spec((D_MODEL, D_FF)),
                  _const_spec((D_FF, D_MODEL))],
        out_specs=pl.BlockSpec((tm, D_MODEL), lambda i: (i, 0)),
        out_shape=jax.ShapeDtypeStruct((n, D_MODEL), F32),
        scratch_shapes=[pltpu.VMEM((tm, D_FF), BF16)],
        compiler_params=_params(1),
        name="ffn_half",
    )(x, g2, wg, wu, wd)


def _mix_out_kernel(n_parts, x_ref, g_ref, *refs):
    parts = refs[:n_parts]
    ws = refs[n_parts:2 * n_parts]
    o_ref = refs[2 * n_parts]
    y = _dot(parts[0][...], ws[0][...])
    for p, w in zip(parts[1:], ws[1:]):
        y = y + _dot(p[...], w[...])
    o_ref[...] = x_ref[...] + _rms(y, g_ref[...])


def _mix_out(x, g, parts, ws, tm):
    n = x.shape[0]
    in_specs = [pl.BlockSpec((tm, D_MODEL), lambda i: (i, 0)), _const_spec((1, D_MODEL))]
    in_specs += [pl.BlockSpec((tm, p.shape[1]), lambda i: (i, 0)) for p in parts]
    in_specs += [_const_spec(w.shape) for w in ws]
    return pl.pallas_call(
        functools.partial(_mix_out_kernel, len(parts)),
        grid=(n // tm,),
        in_specs=in_specs,
        out_specs=pl.BlockSpec((tm, D_MODEL), lambda i: (i, 0)),
        out_shape=jax.ShapeDtypeStruct((n, D_MODEL), F32),
        compiler_params=_params(1),
        name="mix_out",
    )(x, g, *parts, *ws)


def _pool_mix(sums, pool_in, inv_cnt, pw_ref, ps_ref):
    outs = []
    for g in range(len(POOL_WINDOWS)):
        cols = slice(g * POOL_GROUP, (g + 1) * POOL_GROUP)
        pooled = sums[g] * inv_cnt[g] - pool_in[:, cols]
        outs.append(_dot(pooled.astype(BF16), pw_ref[g]) * ps_ref[:, cols])
    return jnp.concatenate(outs, axis=1)


def _even_in_kernel(x_ref, g_ref, w_ref, pw_ref, ps_ref,
                    pool_ref, q_ref, kb_ref, vb_ref, kf_ref, vf_ref, hist_ref, ext_ref):
    t = pl.program_id(1)
    tm = x_ref.shape[1]
    h = _rms(x_ref[0], g_ref[...]).astype(BF16)
    pos = t * tm + lax.broadcasted_iota(jnp.int32, (tm, 1), 0) - PAD
    real = pos >= 0

    pool_in = jnp.where(real, _dot(h, w_ref[:, 0:POOL_WIDTH]), 0.0)
    q = _dot(h, w_ref[:, POOL_WIDTH:POOL_WIDTH + SB_WIDTH])
    k = _dot(h, w_ref[:, POOL_WIDTH + SB_WIDTH:POOL_WIDTH + 2 * SB_WIDTH])
    v = _dot(h, w_ref[:, POOL_WIDTH + 2 * SB_WIDTH:POOL_WIDTH + 3 * SB_WIDTH])
    q_ref[0] = (q * (SB_HEAD_DIM ** -0.5)).astype(BF16)
    kf_ref[0] = k
    vf_ref[0] = v
    kb_ref[0] = k.astype(BF16)
    vb_ref[0] = v.astype(BF16)

    @pl.when(t == 0)
    def _():
        ext_ref[0:HIST_ROWS, :] = jnp.zeros((HIST_ROWS, POOL_WIDTH), F32)

    ext_ref[HIST_ROWS:, :] = pool_in
    sums, inv_cnt = [], []
    for g, w in enumerate(POOL_WINDOWS):
        cols = slice(g * POOL_GROUP, (g + 1) * POOL_GROUP)
        s = pool_in[:, cols]
        for i in range(1, w):
            s = s + ext_ref[HIST_ROWS - i:HIST_ROWS - i + tm, cols]
        sums.append(s)
        inv_cnt.append(1.0 / jnp.clip(pos + 1, 1, w).astype(F32))
    pool_ref[0] = _pool_mix(sums, pool_in, inv_cnt, pw_ref, ps_ref).astype(BF16)
    tail = ext_ref[tm:tm + HIST_ROWS, :]
    ext_ref[0:HIST_ROWS, :] = tail
    hist_ref[0] = tail


def _even_in(x3, g, w_in, pool_w, pool_scale):
    b = x3.shape[0]
    d_in = w_in.shape[1]
    tok = lambda width: pl.BlockSpec((1, TM, width), lambda bi, t: (bi, t, 0))
    sds = lambda width, dt: jax.ShapeDtypeStruct((b, TP, width), dt)
    return pl.pallas_call(
        _even_in_kernel,
        grid=(b, TP // TM),
        in_specs=[tok(D_MODEL), _const_spec((1, D_MODEL)), _const_spec((D_MODEL, d_in)),
                  _const_spec(pool_w.shape), _const_spec((1, POOL_WIDTH))],
        out_specs=[tok(POOL_WIDTH), tok(SB_WIDTH), tok(SB_WIDTH), tok(SB_WIDTH), tok(SB_WIDTH), tok(SB_WIDTH),
                   pl.BlockSpec((1, HIST_ROWS, POOL_WIDTH), lambda bi, t: (bi, 0, 0))],
        out_shape=[sds(POOL_WIDTH, BF16), sds(SB_WIDTH, BF16), sds(SB_WIDTH, BF16), sds(SB_WIDTH, BF16),
                   sds(SB_WIDTH, F32), sds(SB_WIDTH, F32),
                   jax.ShapeDtypeStruct((b, HIST_ROWS, POOL_WIDTH), F32)],
        scratch_shapes=[pltpu.VMEM((TM + HIST_ROWS, POOL_WIDTH), F32)],
        compiler_params=_params(2),
        name="even_in_prompt",
    )(x3, g, w_in, pool_w, pool_scale)


def _even_in_dec_kernel(x_ref, g_ref, w_ref, pw_ref, ps_ref, hist_ref,
                        pool_ref, q_ref, k_ref, v_ref, pin_ref):
    h = _rms(x_ref[...], g_ref[...]).astype(BF16)
    pool_in = _dot(h, w_ref[:, 0:POOL_WIDTH])
    q = _dot(h, w_ref[:, POOL_WIDTH:POOL_WIDTH + SB_WIDTH])
    k_ref[...] = _dot(h, w_ref[:, POOL_WIDTH + SB_WIDTH:POOL_WIDTH + 2 * SB_WIDTH])
    v_ref[...] = _dot(h, w_ref[:, POOL_WIDTH + 2 * SB_WIDTH:POOL_WIDTH + 3 * SB_WIDTH])
    q_ref[...] = (q * (SB_HEAD_DIM ** -0.5)).astype(BF16)
    pin_ref[...] = pool_in
    sums, inv_cnt = [], []
    for g, w in enumerate(POOL_WINDOWS):
        s = pool_in[:, g * POOL_GROUP:(g + 1) * POOL_GROUP]
        for i in range(1, w):
            base = (POOL_HIST - i) * POOL_WIDTH + g * POOL_GROUP
            s = s + hist_ref[:, base:base + POOL_GROUP]
        sums.append(s)
        inv_cnt.append(1.0 / w)
    pool_ref[...] = _pool_mix(sums, pool_in, inv_cnt, pw_ref, ps_ref).astype(BF16)


def _even_in_dec(x, g, w_in, pool_w, pool_scale, hist2d):
    n = x.shape[0]
    d_in = w_in.shape[1]
    full = lambda shape: pl.BlockSpec(shape, lambda i: (0,) * len(shape))
    return pl.pallas_call(
        _even_in_dec_kernel,
        grid=(1,),
        in_specs=[full((n, D_MODEL)), full((1, D_MODEL)), full((D_MODEL, d_in)), full(pool_w.shape),
                  full((1, POOL_WIDTH)), full(hist2d.shape)],
        out_specs=[full((n, POOL_WIDTH)), full((n, SB_WIDTH)), full((n, SB_WIDTH)), full((n, SB_WIDTH)),
                   full((n, POOL_WIDTH))],
        out_shape=[jax.ShapeDtypeStruct((n, POOL_WIDTH), BF16), jax.ShapeDtypeStruct((n, SB_WIDTH), BF16),
                   jax.ShapeDtypeStruct((n, SB_WIDTH), F32), jax.ShapeDtypeStruct((n, SB_WIDTH), F32),
                   jax.ShapeDtypeStruct((n, POOL_WIDTH), F32)],
        compiler_params=_params(1),
        name="even_in_decode",
    )(x, g, w_in, pool_w, pool_scale, hist2d)


def _sb_prompt_kernel(bias_ref, q_ref, k_ref, v_ref, o_ref, acc_ref, carry_ref):
    p = pl.program_id(1)
    acc_ref[...] = jnp.zeros_like(acc_ref)
    carry_ref[...] = jnp.zeros_like(carry_ref)
    m2s = _cumsum_matrix()
    lane = lax.broadcasted_iota(jnp.int32, (1, BLK), 1)
    first = lane < SB_HEAD_DIM
    lane2 = lax.broadcasted_iota(jnp.int32, (1, 2 * BLK), 1)
    bias_row = jnp.where(lane2 < BLK, bias_ref[2 * p], bias_ref[2 * p + 1])
    key_in_blk = lane2 & (BLK - 1)
    row_in_blk = lax.broadcasted_iota(jnp.int32, (BLK, 1), 0)

    def kv_block(j):
        rows = pl.ds(pl.multiple_of(j * BLK, BLK), BLK)
        kj = k_ref[0, rows, :].astype(F32)
        vj = v_ref[0, rows, :].astype(F32)
        kcat = jnp.concatenate([jnp.where(first, kj, 0.0), jnp.where(first, 0.0, kj)], axis=0)
        vcat = jnp.concatenate([jnp.where(first, vj, 0.0), jnp.where(first, 0.0, vj)], axis=0)
        return kcat.astype(BF16), vcat.astype(BF16)

    def q_block(i, j, kcat, vcat, masked):
        rows = pl.ds(pl.multiple_of(i * BLK, BLK), BLK)
        z = _dot_nt(q_ref[0, rows, :], kcat) + bias_row
        lm = _neg_softplus(z)
        if masked:
            kidx = j * BLK + key_in_blk
            valid = (kidx < i * BLK + row_in_blk) & (kidx >= PAD)
            lm = jnp.where(valid, lm, 0.0)
        cs = []
        for hh in range(2):
            cols = slice(hh * BLK, (hh + 1) * BLK)
            c, carry_ref[rows, cols] = _suffix_sums(lm[:, cols], carry_ref[rows, cols], m2s)
            cs.append(c)
        w = jnp.exp(z + jnp.concatenate(cs, axis=1))
        if masked:
            w = jnp.where(valid, w, 0.0)
        acc_ref[rows, :] += _dot(w.astype(BF16), vcat)

    def j_body(jj, _):
        j = NB - 1 - jj
        kcat, vcat = kv_block(j)
        q_block(j, j, kcat, vcat, True)

        def i_body(i, _):
            q_block(i, j, kcat, vcat, False)
            return 0

        lax.fori_loop(j + 1, NB, i_body, 0)
        return 0

    lax.fori_loop(0, NB - 1, j_body, 0)

    kcat0, vcat0 = kv_block(0)

    def i0_body(i, _):
        q_block(i, 0, kcat0, vcat0, True)
        return 0

    lax.fori_loop(0, NB, i0_body, 0)
    o_ref[0] = acc_ref[...].astype(BF16)


def _sb_prompt(q, k, v, bias):
    b = q.shape[0]
    spec = pl.BlockSpec((1, TP, BLK), lambda bi, p: (bi, 0, p))
    return pl.pallas_call(
        _sb_prompt_kernel,
        grid=(b, SB_WIDTH // BLK),
        in_specs=[pl.BlockSpec(memory_space=pltpu.SMEM), spec, spec, spec],
        out_specs=spec,
        out_shape=jax.ShapeDtypeStruct((b, TP, SB_WIDTH), BF16),
        scratch_shapes=[pltpu.VMEM((TP, BLK), F32), pltpu.VMEM((TP, 2 * BLK), F32)],
        compiler_params=_params(2),
        name="sb_prompt",
    )(bias, q, k, v)


PAGES_PER_STEP = 8


def _sb_decode_kernel(pt_ref, q_ref, bias_ref, *refs):
    k_refs = refs[:PAGES_PER_STEP]
    v_refs = refs[PAGES_PER_STEP:2 * PAGES_PER_STEP]
    o_ref, acc_ref, carry_ref = refs[2 * PAGES_PER_STEP:]
    i = pl.program_id(1)

    @pl.when(i == 0)
    def _():
        acc_ref[...] = jnp.zeros_like(acc_ref)
        carry_ref[...] = jnp.zeros_like(carry_ref)

    head = lax.broadcasted_iota(jnp.int32, (SB_HEADS, SB_WIDTH), 0)
    col = lax.broadcasted_iota(jnp.int32, (SB_HEADS, SB_WIDTH), 1)
    own = (col // SB_HEAD_DIM) == head
    qbd = jnp.where(own, jnp.broadcast_to(q_ref[0].astype(F32), (SB_HEADS, SB_WIDTH)), 0.0).astype(BF16)
    m2s = _cumsum_matrix()
    bias = bias_ref[...]
    carry = carry_ref[...]
    acc = acc_ref[...]
    for r in range(PAGES_PER_STEP):
        z = _dot_nt(qbd, k_refs[r][0].astype(BF16)) + bias
        c, carry = _suffix_sums(_neg_softplus(z), carry, m2s)
        w = jnp.exp(z + c)
        acc = acc + _dot(w.astype(BF16), v_refs[r][0].astype(BF16))
    carry_ref[...] = carry
    acc_ref[...] = acc

    @pl.when(i == pl.num_programs(1) - 1)
    def _():
        o_ref[0] = jnp.sum(jnp.where(own, acc, 0.0), axis=0, keepdims=True).astype(BF16)


def _sb_decode(page_table, q3, bias_b, cache_k, cache_v):
    nb, n_pages = page_table.shape
    steps = n_pages // PAGES_PER_STEP

    def page_spec(r):
        return pl.BlockSpec((1, PAGE_SIZE, SB_WIDTH),
                            lambda b, i, pt: (pt[b, n_pages - 1 - (i * PAGES_PER_STEP + r)], 0, 0))

    grid_spec = pltpu.PrefetchScalarGridSpec(
        num_scalar_prefetch=1,
        grid=(nb, steps),
        in_specs=[pl.BlockSpec((1, 1, SB_WIDTH), lambda b, i, pt: (b, 0, 0)),
                  pl.BlockSpec((SB_HEADS, BLK), lambda b, i, pt: (0, 0))]
                 + [page_spec(r) for r in range(PAGES_PER_STEP)] * 2,
        out_specs=pl.BlockSpec((1, 1, SB_WIDTH), lambda b, i, pt: (b, 0, 0)),
        scratch_shapes=[pltpu.VMEM((SB_HEADS, SB_WIDTH), F32), pltpu.VMEM((SB_HEADS, BLK), F32)],
    )
    return pl.pallas_call(
        _sb_decode_kernel,
        grid_spec=grid_spec,
        out_shape=jax.ShapeDtypeStruct((nb, 1, SB_WIDTH), BF16),
        compiler_params=_params(2),
        name="sb_decode",
    )(page_table, q3, bias_b, *([cache_k] * PAGES_PER_STEP), *([cache_v] * PAGES_PER_STEP))


def _odd_in_kernel(pad, x_ref, g_ref, w_ref, cos_ref, sin_ref, q_ref, k_ref, v_ref, sg_ref):
    t = pl.program_id(1)
    tm = x_ref.shape[1]
    h = _rms(x_ref[0], g_ref[...]).astype(BF16)
    dq = RET_HEADS * RET_QK_DIM
    dv = RET_HEADS * RET_V_DIM
    cos = cos_ref[...]
    sin = sin_ref[...]
    half = RET_QK_DIM // 2
    if pad:
        real = (t * tm + lax.broadcasted_iota(jnp.int32, (tm, 1), 0)) >= pad
    for hd in range(RET_HEADS):
        for which, out_ref, scale in ((0, q_ref, 1.0), (1, k_ref, RET_QK_DIM ** -0.5)):
            base = which * dq + hd * RET_QK_DIM
            a = _dot(h, w_ref[:, base:base + RET_QK_DIM])
            x1, x2 = a[:, :half], a[:, half:]
            rot = jnp.concatenate([x1 * cos - x2 * sin, x1 * sin + x2 * cos], axis=1) * scale
            if pad and which == 1:
                rot = jnp.where(real, rot, 0.0)
            out_ref[0, :, hd * RET_QK_DIM:(hd + 1) * RET_QK_DIM] = rot.astype(BF16)
        cols = slice(hd * RET_V_DIM, (hd + 1) * RET_V_DIM)
        v = _dot(h, w_ref[:, 2 * dq + hd * RET_V_DIM:2 * dq + (hd + 1) * RET_V_DIM])
        if pad:
            v = jnp.where(real, v, 0.0)
        v_ref[0, :, cols] = v.astype(BF16)
        gt = _dot(h, w_ref[:, 2 * dq + dv + hd * RET_V_DIM:2 * dq + dv + (hd + 1) * RET_V_DIM])
        sg_ref[0, :, cols] = gt * jax.nn.sigmoid(gt)


def _odd_in(x3, g, w_in, cos, sin, tm, pad):
    b, t_len, _ = x3.shape
    dq = RET_HEADS * RET_QK_DIM
    dv = RET_HEADS * RET_V_DIM
    tok = lambda width: pl.BlockSpec((1, tm, width), lambda bi, t: (bi, t, 0))
    tab = pl.BlockSpec((tm, RET_QK_DIM // 2), lambda bi, t: (t, 0))
    sds = lambda width, dt: jax.ShapeDtypeStruct((b, t_len, width), dt)
    return pl.pallas_call(
        functools.partial(_odd_in_kernel, pad),
        grid=(b, t_len // tm),
        in_specs=[tok(D_MODEL), _const_spec((1, D_MODEL)), _const_spec(w_in.shape), tab, tab],
        out_specs=[tok(dq), tok(dq), tok(dv), tok(dv)],
        out_shape=[sds(dq, BF16), sds(dq, BF16), sds(dv, BF16), sds(dv, F32)],
        compiler_params=_params(2),
        name="odd_in",
    )(x3, g, w_in, cos, sin)


def _ret_prompt_kernel(lg_ref, q_ref, k_ref, v_ref, sg_ref, o_ref, st_ref, state_ref):
    hd = pl.program_id(1)
    c = pl.program_id(2)

    @pl.when(c == 0)
    def _():
        state_ref[...] = jnp.zeros_like(state_ref)

    lg = lg_ref[hd]
    row = lax.broadcasted_iota(jnp.int32, (BLK, BLK), 0)
    colm = lax.broadcasted_iota(jnp.int32, (BLK, BLK), 1)
    diff = (row - colm).astype(F32)
    decay = jnp.where(diff >= 0, jnp.exp(lg * jnp.maximum(diff, 0.0)), 0.0)
    idx = lax.broadcasted_iota(jnp.int32, (BLK, 1), 0).astype(F32)
    q_dec = jnp.exp(lg * (idx + 1.0))
    k_dec = jnp.exp(lg * (BLK - 1.0 - idx))
    chunk_dec = jnp.exp(jnp.full((1, 1), BLK, F32) * lg)

    q = q_ref[0]
    k = k_ref[0]
    v = v_ref[0]
    state = state_ref[...]
    scores = (_dot_nt(q, k) * decay).astype(BF16)
    o = _dot(scores, v) + _dot(q, state.astype(BF16)) * q_dec
    new_state = state * chunk_dec + _dot_tn((k.astype(F32) * k_dec).astype(BF16), v)
    state_ref[...] = new_state
    on = o * lax.rsqrt(jnp.mean(o * o, axis=-1, keepdims=True) + EPS)
    o_ref[0] = (sg_ref[0] * on).astype(BF16)

    @pl.when(c == pl.num_programs(2) - 1)
    def _():
        st_ref[0, 0] = new_state


def _ret_prompt(lg, q, k, v, sg):
    b = q.shape[0]
    qk_spec = pl.BlockSpec((1, BLK, RET_QK_DIM), lambda bi, hd, c: (bi, c, hd))
    v_spec = pl.BlockSpec((1, BLK, RET_V_DIM), lambda bi, hd, c: (bi, c, hd))
    return pl.pallas_call(
        _ret_prompt_kernel,
        grid=(b, RET_HEADS, NB),
        in_specs=[pl.BlockSpec(memory_space=pltpu.SMEM), qk_spec, qk_spec, v_spec, v_spec],
        out_specs=[v_spec, pl.BlockSpec((1, 1, RET_QK_DIM, RET_V_DIM), lambda bi, hd, c: (bi, hd, 0, 0))],
        out_shape=[jax.ShapeDtypeStruct((b, TP, RET_HEADS * RET_V_DIM), BF16),
                   jax.ShapeDtypeStruct((b, RET_HEADS, RET_QK_DIM, RET_V_DIM), F32)],
        scratch_shapes=[pltpu.VMEM((RET_QK_DIM, RET_V_DIM), F32)],
        compiler_params=_params(3),
        name="retention_prompt",
    )(lg, q, k, v, sg)


def _ret_decode_kernel(lg_ref, q_ref, k_ref, v_ref, sg_ref, s_ref, o_ref, st_ref):
    lg = lg_ref[pl.program_id(1)]
    gamma = jnp.exp(jnp.full((1, 1), 1.0, F32) * lg)
    q = q_ref[0]
    k = k_ref[0]
    v = v_ref[0]
    state = s_ref[0, 0]
    row0 = lax.broadcasted_iota(jnp.int32, (8, 1), 0) == 0
    qf = q.astype(F32)
    kf = k.astype(F32)
    q8 = jnp.where(row0, jnp.broadcast_to(qf, (8, RET_QK_DIM)), 0.0).astype(BF16)
    k8 = jnp.where(row0, jnp.broadcast_to(kf, (8, RET_QK_DIM)), 0.0).astype(BF16)
    v8 = jnp.broadcast_to(v.astype(F32), (8, RET_V_DIM)).astype(BF16)
    score = jnp.sum(qf * kf, axis=-1, keepdims=True).astype(BF16).astype(F32)
    cross = _dot(q8, state.astype(BF16))[0:1, :] * gamma
    o = score * v.astype(F32) + cross
    st_ref[0, 0] = state * gamma + _dot_tn(k8, v8)
    on = o * lax.rsqrt(jnp.mean(o * o, axis=-1, keepdims=True) + EPS)
    o_ref[0] = (sg_ref[0] * on).astype(BF16)


def _ret_decode(lg, q, k, v, sg, state):
    nb = q.shape[0]
    qk_spec = pl.BlockSpec((1, 1, RET_QK_DIM), lambda bi, hd: (bi, 0, hd))
    v_spec = pl.BlockSpec((1, 1, RET_V_DIM), lambda bi, hd: (bi, 0, hd))
    st_spec = pl.BlockSpec((1, 1, RET_QK_DIM, RET_V_DIM), lambda bi, hd: (bi, hd, 0, 0))
    return pl.pallas_call(
        _ret_decode_kernel,
        grid=(nb, RET_HEADS),
        in_specs=[pl.BlockSpec(memory_space=pltpu.SMEM), qk_spec, qk_spec, v_spec, v_spec, st_spec],
        out_specs=[v_spec, st_spec],
        out_shape=[jax.ShapeDtypeStruct((nb, 1, RET_HEADS * RET_V_DIM), BF16),
                   jax.ShapeDtypeStruct(state.shape, F32)],
        compiler_params=_params(2),
        name="retention_decode",
    )(lg, q, k, v, sg, state)


def _rope_tables(pos):
    half = RET_QK_DIM // 2
    inv = ROPE_BASE ** (-jnp.arange(half, dtype=F32) / half)
    ang = pos.astype(F32)[:, None] * inv[None, :]
    return jnp.cos(ang), jnp.sin(ang)


def kernel(x_prompt, x_sample, cache_k, cache_v, state_pool, state_ret, page_table, meta_tokens, norm_g,
           ffn_w_gate, ffn_w_up, ffn_w_down, even_w_in, even_w_out, pool_w, pool_scale, sb_bias,
           odd_w_in, odd_w_out):
    b = x_prompt.shape[0]
    db = x_sample.shape[0]
    depth = norm_g.shape[0]
    n_phys = cache_k.shape[1]

    xp = jnp.concatenate([jnp.zeros((b, PAD, D_MODEL), F32),
                          jnp.broadcast_to(meta_tokens[None].astype(F32), (b, N_META, D_MODEL)),
                          x_prompt], axis=1).reshape(b * TP, D_MODEL)
    xs = x_sample.reshape(db, D_MODEL)

    lg = jnp.log1p(-jnp.exp2(-5.0 - jnp.arange(RET_HEADS, dtype=F32)))
    cos_p, sin_p = _rope_tables(jnp.arange(TP) - PAD)
    cos_s, sin_s = _rope_tables(jnp.full((db,), PAST_LEN))

    wg = ffn_w_gate.astype(BF16)
    wu = ffn_w_up.astype(BF16)
    wd = ffn_w_down.astype(BF16)

    def ffn(x, layer, half, tm):
        g2 = norm_g[layer, 4 * half:4 * half + 2]
        return _ffn(x, g2, wg[layer, half], wu[layer, half], wd[layer, half], tm)

    pool_p, pool_s, k_p, v_p, k_s, v_s, ret_p, ret_s = [], [], [], [], [], [], [], []
    for layer in range(depth):
        xp = ffn(xp, layer, 0, TM)
        xs = ffn(xs, layer, 0, db)
        g_in = norm_g[layer, 2:3]
        g_out = norm_g[layer, 3:4]
        if layer % 2 == 0:
            e = layer // 2
            w_in = even_w_in[e].astype(BF16)
            w_out = even_w_out[e].astype(BF16)
            pw = pool_w[e].astype(BF16)
            ps = pool_scale[e][None]
            pool_o, q, kb, vb, kf, vf, hist = _even_in(xp.reshape(b, TP, D_MODEL), g_in, w_in, pw, ps)
            sb_o = _sb_prompt(q, kb, vb, sb_bias[e])
            xp = _mix_out(xp, g_out, [pool_o.reshape(b * TP, POOL_WIDTH), sb_o.reshape(b * TP, SB_WIDTH)],
                          [w_out[:POOL_WIDTH], w_out[POOL_WIDTH:]], TM)
            pool_p.append(hist[:, HIST_ROWS - POOL_HIST:])
            k_p.append(kf[:, PAD:].reshape(b, T_REAL, SB_HEADS, SB_HEAD_DIM))
            v_p.append(vf[:, PAD:].reshape(b, T_REAL, SB_HEADS, SB_HEAD_DIM))

            hist_s = state_pool[e]
            pool_os, qs, ks, vs, pin_s = _even_in_dec(xs, g_in, w_in, pw, ps,
                                                     hist_s.reshape(db, POOL_HIST * POOL_WIDTH))
            sb_os = _sb_decode(page_table, qs.reshape(db, 1, SB_WIDTH),
                               jnp.broadcast_to(sb_bias[e][:, None], (SB_HEADS, BLK)),
                               cache_k[e].reshape(n_phys, PAGE_SIZE, SB_WIDTH),
                               cache_v[e].reshape(n_phys, PAGE_SIZE, SB_WIDTH))
            xs = _mix_out(xs, g_out, [pool_os, sb_os.reshape(db, SB_WIDTH)],
                          [w_out[:POOL_WIDTH], w_out[POOL_WIDTH:]], db)
            pool_s.append(jnp.concatenate([hist_s[:, 1:], pin_s[:, None]], axis=1))
            k_s.append(ks.reshape(db, 1, SB_HEADS, SB_HEAD_DIM))
            v_s.append(vs.reshape(db, 1, SB_HEADS, SB_HEAD_DIM))
        else:
            o = layer // 2
            w_in = odd_w_in[o].astype(BF16)
            w_out = odd_w_out[o].astype(BF16)
            q, k, v, sg = _odd_in(xp.reshape(b, TP, D_MODEL), g_in, w_in, cos_p, sin_p, TM, PAD)
            gated, st = _ret_prompt(lg, q, k, v, sg)
            xp = _mix_out(xp, g_out, [gated.reshape(b * TP, RET_HEADS * RET_V_DIM)], [w_out], TM)
            ret_p.append(st)

            qs, ks, vs, sgs = _odd_in(xs.reshape(1, db, D_MODEL), g_in, w_in, cos_s, sin_s, db, 0)
            to_rows = lambda a: a.reshape(db, 1, a.shape[-1])
            gated_s, st_s = _ret_decode(lg, to_rows(qs), to_rows(ks), to_rows(vs), to_rows(sgs), state_ret[o])
            xs = _mix_out(xs, g_out, [gated_s.reshape(db, RET_HEADS * RET_V_DIM)], [w_out], db)
            ret_s.append(st_s)
        xp = ffn(xp, layer, 1, TM)
        xs = ffn(xs, layer, 1, db)

    y_prompt = xp.reshape(b, TP, D_MODEL)[:, TP - SEQ:]
    y_sample = xs.reshape(db, 1, D_MODEL)
    return (y_prompt, y_sample, jnp.stack(pool_p), jnp.stack(pool_s), jnp.stack(k_p), jnp.stack(v_p),
            jnp.stack(k_s), jnp.stack(v_s), jnp.stack(ret_p), jnp.stack(ret_s))
```

```python
import functools
import math

import jax
import jax.numpy as jnp
from jax import lax
from jax.experimental import pallas as pl
from jax.experimental.pallas import tpu as pltpu

F32 = jnp.float32
BF16 = jnp.bfloat16

D_MODEL = 1024
SEQ = 8192
PAST_LEN = 16384
PAGE_SIZE = 128
N_META = 16
POOL_WIDTH = D_MODEL // 2
POOL_WINDOWS = (2, 4, 8, 16)
POOL_GROUP = POOL_WIDTH // len(POOL_WINDOWS)
POOL_HIST = max(POOL_WINDOWS) - 1
SB_HEADS = 8
SB_HEAD_DIM = (D_MODEL // 2) // SB_HEADS
SB_WIDTH = SB_HEADS * SB_HEAD_DIM
RET_HEADS = 4
RET_QK_DIM = D_MODEL // RET_HEADS
RET_V_DIM = 2 * D_MODEL // RET_HEADS
ROPE_BASE = 10000.0
D_FF = 2816
EPS = 1e-6

BLK = 128
T_REAL = N_META + SEQ
NB = -(-T_REAL // BLK)
TP = NB * BLK
PAD = TP - T_REAL
HIST_ROWS = 16
TM = 640
FF_CHUNK = 256
BLOCKS_PER_GROUP = 13
Q_GROUP = BLOCKS_PER_GROUP * BLK
PAD_LOGIT = 1e30
VMEM_LIMIT = 56 * 1024 * 1024


def _params(n_grid):
    return pltpu.CompilerParams(dimension_semantics=("arbitrary",) * n_grid,
                                vmem_limit_bytes=VMEM_LIMIT)


def _const_spec(shape):
    nd = len(shape)
    return pl.BlockSpec(shape, lambda *_: (0,) * nd, pipeline_mode=pl.Buffered(1))


def _rms(x, g):
    return x * lax.rsqrt(jnp.mean(x * x, axis=-1, keepdims=True) + EPS) * g


def _dot(a, b):
    return jnp.dot(a, b, preferred_element_type=F32)


def _dot_nt(a, b):
    return lax.dot_general(a, b, (((1,), (1,)), ((), ())), preferred_element_type=F32)


def _dot_tn(a, b):
    return lax.dot_general(a, b, (((0,), (0,)), ((), ())), preferred_element_type=F32)


def _neg_softplus(z):
    return -(jnp.maximum(z, 0.0) + jnp.log(1.0 + jnp.exp(-jnp.abs(z))))


def _cumsum_matrix():
    r = lax.broadcasted_iota(jnp.int32, (2 * BLK, 2 * BLK), 0) & (BLK - 1)
    c = lax.broadcasted_iota(jnp.int32, (2 * BLK, 2 * BLK), 1)
    return jnp.where((c >= BLK) | (r >= c), 1.0, 0.0).astype(BF16)


def _suffix_sums(lm, carry, m2s):
    hi = lm.astype(BF16)
    lo = (lm - hi.astype(F32)).astype(BF16)
    res = _dot(jnp.concatenate([hi, lo], axis=1), m2s) + jnp.concatenate([carry, carry], axis=1)
    return res[:, :BLK], res[:, BLK:]


def _ffn_kernel(x_ref, g_ref, wg_ref, wu_ref, wd_ref, o_ref, act_ref):
    x = x_ref[...]
    h = _rms(x, g_ref[0:1, :]).astype(BF16)
    for c in range(D_FF // FF_CHUNK):
        sl = slice(c * FF_CHUNK, (c + 1) * FF_CHUNK)
        gate = _dot(h, wg_ref[:, sl])
        up = _dot(h, wu_ref[:, sl])
        act_ref[:, sl] = (gate * jax.nn.sigmoid(gate) * up).astype(BF16)
    f = _dot(act_ref[...], wd_ref[...])
    o_ref[...] = x + 0.5 * _rms(f, g_ref[1:2, :])


def _ffn(x, g2, wg, wu, wd, tm):
    n = x.shape[0]
    return pl.pallas_call(
        _ffn_kernel,
        grid=(n // tm,),
        in_specs=[pl.BlockSpec((tm, D_MODEL), lambda i: (i, 0)),
                  _const_spec((2, D_MODEL)),
                  _const_spec((D_MODEL, D_FF)),
                  _const_spec((D_MODEL, D_FF)),
                  _const_spec((D_FF, D_MODEL))],
        out_specs=pl.BlockSpec((tm, D_MODEL), lambda i: (i, 0)),
        out_shape=jax.ShapeDtypeStruct((n, D_MODEL), F32),
        scratch_shapes=[pltpu.VMEM((tm, D_FF), BF16)],
        compiler_params=_params(1),
        name="ffn_half",
    )(x, g2, wg, wu, wd)


def _mix_out_kernel(n_parts, x_ref, g_ref, *refs):
    parts = refs[:n_parts]
    ws = refs[n_parts:2 * n_parts]
    o_ref = refs[2 * n_parts]
    y = _dot(parts[0][...], ws[0][...])
    for p, w in zip(parts[1:], ws[1:]):
        y = y + _dot(p[...], w[...])
    o_ref[...] = x_ref[...] + _rms(y, g_ref[...])


def _mix_out(x, g, parts, ws, tm):
    n = x.shape[0]
    in_specs = [pl.BlockSpec((tm, D_MODEL), lambda i: (i, 0)), _const_spec((1, D_MODEL))]
    in_specs += [pl.BlockSpec((tm, p.shape[1]), lambda i: (i, 0)) for p in parts]
    in_specs += [_const_spec(w.shape) for w in ws]
    return pl.pallas_call(
        functools.partial(_mix_out_kernel, len(parts)),
        grid=(n // tm,),
        in_specs=in_specs,
        out_specs=pl.BlockSpec((tm, D_MODEL), lambda i: (i, 0)),
        out_shape=jax.ShapeDtypeStruct((n, D_MODEL), F32),
        compiler_params=_params(1),
        name="mix_out",
    )(x, g, *parts, *ws)


def _pool_mix(sums, pool_in, inv_cnt, pw_ref, ps_ref):
    outs = []
    for g in range(len(POOL_WINDOWS)):
        cols = slice(g * POOL_GROUP, (g + 1) * POOL_GROUP)
        pooled = sums[g] * inv_cnt[g] - pool_in[:, cols]
        outs.append(_dot(pooled.astype(BF16), pw_ref[g]) * ps_ref[:, cols])
    return jnp.concatenate(outs, axis=1)


def _even_in_kernel(x_ref, g_ref, w_ref, pw_ref, ps_ref,
                    pool_ref, q_ref, kb_ref, vb_ref, kf_ref, vf_ref, hist_ref, ext_ref):
    t = pl.program_id(1)
    tm = x_ref.shape[1]
    h = _rms(x_ref[0], g_ref[...]).astype(BF16)
    pos = t * tm + lax.broadcasted_iota(jnp.int32, (tm, 1), 0) - PAD
    real = pos >= 0

    pool_in = jnp.where(real, _dot(h, w_ref[:, 0:POOL_WIDTH]), 0.0)
    q = _dot(h, w_ref[:, POOL_WIDTH:POOL_WIDTH + SB_WIDTH])
    k = _dot(h, w_ref[:, POOL_WIDTH + SB_WIDTH:POOL_WIDTH + 2 * SB_WIDTH])
    v = _dot(h, w_ref[:, POOL_WIDTH + 2 * SB_WIDTH:POOL_WIDTH + 3 * SB_WIDTH])
    q_ref[0] = (q * -(SB_HEAD_DIM ** -0.5)).astype(BF16)
    kf_ref[0] = k
    vf_ref[0] = v
    kb_ref[0] = k.astype(BF16)
    vb_ref[0] = v.astype(BF16)

    @pl.when(t == 0)
    def _():
        ext_ref[0:HIST_ROWS, :] = jnp.zeros((HIST_ROWS, POOL_WIDTH), F32)

    ext_ref[HIST_ROWS:, :] = pool_in
    sums, inv_cnt = [], []
    for g, w in enumerate(POOL_WINDOWS):
        cols = slice(g * POOL_GROUP, (g + 1) * POOL_GROUP)
        s = pool_in[:, cols]
        for i in range(1, w):
            s = s + ext_ref[HIST_ROWS - i:HIST_ROWS - i + tm, cols]
        sums.append(s)
        inv_cnt.append(1.0 / jnp.clip(pos + 1, 1, w).astype(F32))
    pool_ref[0] = _pool_mix(sums, pool_in, inv_cnt, pw_ref, ps_ref).astype(BF16)
    tail = ext_ref[tm:tm + HIST_ROWS, :]
    ext_ref[0:HIST_ROWS, :] = tail
    hist_ref[0] = tail


def _even_in(x3, g, w_in, pool_w, pool_scale):
    b = x3.shape[0]
    d_in = w_in.shape[1]
    tok = lambda width: pl.BlockSpec((1, TM, width), lambda bi, t: (bi, t, 0))
    sds = lambda width, dt: jax.ShapeDtypeStruct((b, TP, width), dt)
    return pl.pallas_call(
        _even_in_kernel,
        grid=(b, TP // TM),
        in_specs=[tok(D_MODEL), _const_spec((1, D_MODEL)), _const_spec((D_MODEL, d_in)),
                  _const_spec(pool_w.shape), _const_spec((1, POOL_WIDTH))],
        out_specs=[tok(POOL_WIDTH), tok(SB_WIDTH), tok(SB_WIDTH), tok(SB_WIDTH), tok(SB_WIDTH), tok(SB_WIDTH),
                   pl.BlockSpec((1, HIST_ROWS, POOL_WIDTH), lambda bi, t: (bi, 0, 0))],
        out_shape=[sds(POOL_WIDTH, BF16), sds(SB_WIDTH, BF16), sds(SB_WIDTH, BF16), sds(SB_WIDTH, BF16),
                   sds(SB_WIDTH, F32), sds(SB_WIDTH, F32),
                   jax.ShapeDtypeStruct((b, HIST_ROWS, POOL_WIDTH), F32)],
        scratch_shapes=[pltpu.VMEM((TM + HIST_ROWS, POOL_WIDTH), F32)],
        compiler_params=_params(2),
        name="even_in_prompt",
    )(x3, g, w_in, pool_w, pool_scale)


def _even_in_dec_kernel(x_ref, g_ref, w_ref, pw_ref, ps_ref, hist_ref,
                        pool_ref, q_ref, k_ref, v_ref, pin_ref):
    h = _rms(x_ref[...], g_ref[...]).astype(BF16)
    pool_in = _dot(h, w_ref[:, 0:POOL_WIDTH])
    q = _dot(h, w_ref[:, POOL_WIDTH:POOL_WIDTH + SB_WIDTH])
    k_ref[...] = _dot(h, w_ref[:, POOL_WIDTH + SB_WIDTH:POOL_WIDTH + 2 * SB_WIDTH])
    v_ref[...] = _dot(h, w_ref[:, POOL_WIDTH + 2 * SB_WIDTH:POOL_WIDTH + 3 * SB_WIDTH])
    q_ref[...] = (q * (SB_HEAD_DIM ** -0.5)).astype(BF16)
    pin_ref[...] = pool_in
    sums, inv_cnt = [], []
    for g, w in enumerate(POOL_WINDOWS):
        s = pool_in[:, g * POOL_GROUP:(g + 1) * POOL_GROUP]
        for i in range(1, w):
            base = (POOL_HIST - i) * POOL_WIDTH + g * POOL_GROUP
            s = s + hist_ref[:, base:base + POOL_GROUP]
        sums.append(s)
        inv_cnt.append(1.0 / w)
    pool_ref[...] = _pool_mix(sums, pool_in, inv_cnt, pw_ref, ps_ref).astype(BF16)


def _even_in_dec(x, g, w_in, pool_w, pool_scale, hist2d):
    n = x.shape[0]
    d_in = w_in.shape[1]
    full = lambda shape: pl.BlockSpec(shape, lambda i: (0,) * len(shape))
    return pl.pallas_call(
        _even_in_dec_kernel,
        grid=(1,),
        in_specs=[full((n, D_MODEL)), full((1, D_MODEL)), full((D_MODEL, d_in)), full(pool_w.shape),
                  full((1, POOL_WIDTH)), full(hist2d.shape)],
        out_specs=[full((n, POOL_WIDTH)), full((n, SB_WIDTH)), full((n, SB_WIDTH)), full((n, SB_WIDTH)),
                   full((n, POOL_WIDTH))],
        out_shape=[jax.ShapeDtypeStruct((n, POOL_WIDTH), BF16), jax.ShapeDtypeStruct((n, SB_WIDTH), BF16),
                   jax.ShapeDtypeStruct((n, SB_WIDTH), F32), jax.ShapeDtypeStruct((n, SB_WIDTH), F32),
                   jax.ShapeDtypeStruct((n, POOL_WIDTH), F32)],
        compiler_params=_params(1),
        name="even_in_decode",
    )(x, g, w_in, pool_w, pool_scale, hist2d)


def _sb_prompt_kernel(bias_ref, q_ref, k_ref, v_ref, o_ref, acc_ref, carry_ref):
    p = pl.program_id(1)
    acc_ref[...] = jnp.zeros_like(acc_ref)
    carry_ref[...] = jnp.zeros_like(carry_ref)
    m2s = _cumsum_matrix()
    lane = lax.broadcasted_iota(jnp.int32, (1, BLK), 1)
    first = lane < SB_HEAD_DIM
    lane2 = lax.broadcasted_iota(jnp.int32, (1, 2 * BLK), 1)
    neg_bias = jnp.where(lane2 < BLK, -bias_ref[2 * p], -bias_ref[2 * p + 1])
    key_in_blk = lane2 & (BLK - 1)
    row_in_grp = lax.broadcasted_iota(jnp.int32, (Q_GROUP, 1), 0)

    def kv_block(j):
        rows = pl.ds(pl.multiple_of(j * BLK, BLK), BLK)
        kj = k_ref[0, rows, :].astype(F32)
        vj = v_ref[0, rows, :].astype(F32)
        kcat = jnp.concatenate([jnp.where(first, kj, 0.0), jnp.where(first, 0.0, kj)], axis=0)
        vcat = jnp.concatenate([jnp.where(first, vj, 0.0), jnp.where(first, 0.0, vj)], axis=0)
        return kcat.astype(BF16), vcat.astype(BF16)

    def q_group(g, j, kcat, vcat, nb_j, masked):
        rows = pl.ds(pl.multiple_of(g * Q_GROUP, Q_GROUP), Q_GROUP)
        u = _dot_nt(q_ref[0, rows, :], kcat) + nb_j
        lm = jnp.minimum(u, 0.0) - jnp.log(1.0 + jnp.exp(-jnp.abs(u)))
        if masked:
            valid = (j * BLK + key_in_blk) < (g * Q_GROUP + row_in_grp)
            lm = jnp.where(valid, lm, 0.0)
        cs = []
        for hh in range(2):
            cols = slice(hh * BLK, (hh + 1) * BLK)
            c, carry_ref[rows, cols] = _suffix_sums(lm[:, cols], carry_ref[rows, cols], m2s)
            cs.append(c)
        w = jnp.exp(jnp.concatenate(cs, axis=1) - u)
        if masked:
            w = jnp.where(valid, w, 0.0)
        acc_ref[rows, :] += _dot(w.astype(BF16), vcat)

    def j_body(jj, _):
        j = NB - 1 - jj
        kcat, vcat = kv_block(j)
        nb_j = neg_bias + jnp.where(j * BLK + key_in_blk < PAD, PAD_LOGIT, 0.0)
        g0 = j // BLOCKS_PER_GROUP
        q_group(g0, j, kcat, vcat, nb_j, True)

        def g_body(g, _):
            q_group(g, j, kcat, vcat, nb_j, False)
            return 0

        lax.fori_loop(g0 + 1, TP // Q_GROUP, g_body, 0)
        return 0

    lax.fori_loop(0, NB, j_body, 0)
    o_ref[0] = acc_ref[...].astype(BF16)


def _sb_prompt(q, k, v, bias):
    b = q.shape[0]
    spec = pl.BlockSpec((1, TP, BLK), lambda bi, p: (bi, 0, p))
    return pl.pallas_call(
        _sb_prompt_kernel,
        grid=(b, SB_WIDTH // BLK),
        in_specs=[pl.BlockSpec(memory_space=pltpu.SMEM), spec, spec, spec],
        out_specs=spec,
        out_shape=jax.ShapeDtypeStruct((b, TP, SB_WIDTH), BF16),
        scratch_shapes=[pltpu.VMEM((TP, BLK), F32), pltpu.VMEM((TP, 2 * BLK), F32)],
        compiler_params=_params(2),
        name="sb_prompt",
    )(bias, q, k, v)


PAGES_PER_STEP = 8


PAGE_COLS = PAGE_SIZE * SB_HEADS
COL_GROUPS = PAGE_COLS // BLK


def _sb_decode_kernel(pt_ref, q_ref, bias_ref, *refs):
    k_refs = refs[:PAGES_PER_STEP]
    v_refs = refs[PAGES_PER_STEP:2 * PAGES_PER_STEP]
    o_ref, acc_ref, carry_ref = refs[2 * PAGES_PER_STEP:]
    i = pl.program_id(1)

    @pl.when(i == 0)
    def _():
        acc_ref[...] = jnp.zeros_like(acc_ref)
        carry_ref[...] = jnp.zeros_like(carry_ref)

    head = lax.broadcasted_iota(jnp.int32, (SB_HEADS, PAGE_COLS), 0)
    col = lax.broadcasted_iota(jnp.int32, (SB_HEADS, PAGE_COLS), 1)
    own = (col & (SB_HEADS - 1)) == head
    q = q_ref[0]
    m2s = _cumsum_matrix()
    bias = bias_ref[...]
    zs = [_dot_nt(q, k_refs[r][...].reshape(PAGE_COLS, SB_HEAD_DIM).astype(BF16)) + bias
          for r in range(PAGES_PER_STEP)]
    lms = [jnp.where(own, _neg_softplus(z), 0.0) for z in zs]
    stacked = jnp.concatenate([lm[:, g * BLK:(g + 1) * BLK] for lm in lms for g in range(COL_GROUPS)], axis=0)
    hi = stacked.astype(BF16)
    lo = (stacked - hi.astype(F32)).astype(BF16)
    res = _dot(jnp.concatenate([hi, lo], axis=1), m2s)

    def blk(r, g, half):
        row0 = (r * COL_GROUPS + g) * SB_HEADS
        return res[row0:row0 + SB_HEADS, half * BLK:(half + 1) * BLK]

    offset = carry_ref[...]
    acc = acc_ref[...]
    for r in range(PAGES_PER_STEP):
        newer = [None] * COL_GROUPS
        run = jnp.zeros((SB_HEADS, BLK), F32)
        for g in reversed(range(COL_GROUPS)):
            newer[g] = run
            run = run + blk(r, g, 1)
        cs = [blk(r, g, 0) + (newer[g] + offset) for g in range(COL_GROUPS)]
        offset = offset + run
        w = jnp.where(own, jnp.exp(zs[r] + jnp.concatenate(cs, axis=1)), 0.0)
        v2 = v_refs[r][...].reshape(PAGE_COLS, SB_HEAD_DIM).astype(BF16)
        acc = acc + _dot(w.astype(BF16), v2)
    carry_ref[...] = offset
    acc_ref[...] = acc

    @pl.when(i == pl.num_programs(1) - 1)
    def _():
        o_ref[0] = acc.astype(BF16)


def _sb_decode(page_table, q3, bias_b, cache_k, cache_v, e):
    nb, n_pages = page_table.shape
    steps = n_pages // PAGES_PER_STEP

    def page_spec(r):
        return pl.BlockSpec((None, None, PAGE_SIZE, SB_HEADS, SB_HEAD_DIM),
                            lambda b, i, pt: (e, pt[b, n_pages - 1 - (i * PAGES_PER_STEP + r)], 0, 0, 0))

    grid_spec = pltpu.PrefetchScalarGridSpec(
        num_scalar_prefetch=1,
        grid=(nb, steps),
        in_specs=[pl.BlockSpec((1, SB_HEADS, SB_HEAD_DIM), lambda b, i, pt: (b, 0, 0)),
                  pl.BlockSpec((SB_HEADS, PAGE_COLS), lambda b, i, pt: (0, 0))]
                 + [page_spec(r) for r in range(PAGES_PER_STEP)] * 2,
        out_specs=pl.BlockSpec((1, SB_HEADS, SB_HEAD_DIM), lambda b, i, pt: (b, 0, 0)),
        scratch_shapes=[pltpu.VMEM((SB_HEADS, SB_HEAD_DIM), F32), pltpu.VMEM((SB_HEADS, BLK), F32)],
    )
    return pl.pallas_call(
        _sb_decode_kernel,
        grid_spec=grid_spec,
        out_shape=jax.ShapeDtypeStruct((nb, SB_HEADS, SB_HEAD_DIM), BF16),
        compiler_params=_params(2),
        name="sb_decode",
    )(page_table, q3, bias_b, *([cache_k] * PAGES_PER_STEP), *([cache_v] * PAGES_PER_STEP))


def _odd_in_kernel(pad, x_ref, g_ref, w_ref, cos_ref, sin_ref, q_ref, k_ref, v_ref, sg_ref):
    t = pl.program_id(1)
    tm = x_ref.shape[1]
    h = _rms(x_ref[0], g_ref[...]).astype(BF16)
    dq = RET_HEADS * RET_QK_DIM
    dv = RET_HEADS * RET_V_DIM
    cos = cos_ref[...]
    sin = sin_ref[...]
    half = RET_QK_DIM // 2
    if pad:
        real = (t * tm + lax.broadcasted_iota(jnp.int32, (tm, 1), 0)) >= pad
    for hd in range(RET_HEADS):
        for which, out_ref, scale in ((0, q_ref, 1.0), (1, k_ref, RET_QK_DIM ** -0.5)):
            base = which * dq + hd * RET_QK_DIM
            a = _dot(h, w_ref[:, base:base + RET_QK_DIM])
            x1, x2 = a[:, :half], a[:, half:]
            rot = jnp.concatenate([x1 * cos - x2 * sin, x1 * sin + x2 * cos], axis=1) * scale
            if pad and which == 1:
                rot = jnp.where(real, rot, 0.0)
            out_ref[0, :, hd * RET_QK_DIM:(hd + 1) * RET_QK_DIM] = rot.astype(BF16)
        cols = slice(hd * RET_V_DIM, (hd + 1) * RET_V_DIM)
        v = _dot(h, w_ref[:, 2 * dq + hd * RET_V_DIM:2 * dq + (hd + 1) * RET_V_DIM])
        if pad:
            v = jnp.where(real, v, 0.0)
        v_ref[0, :, cols] = v.astype(BF16)
        gt = _dot(h, w_ref[:, 2 * dq + dv + hd * RET_V_DIM:2 * dq + dv + (hd + 1) * RET_V_DIM])
        sg_ref[0, :, cols] = gt * jax.nn.sigmoid(gt)


def _odd_in(x3, g, w_in, cos, sin, tm, pad):
    b, t_len, _ = x3.shape
    dq = RET_HEADS * RET_QK_DIM
    dv = RET_HEADS * RET_V_DIM
    tok = lambda width: pl.BlockSpec((1, tm, width), lambda bi, t: (bi, t, 0))
    tab = pl.BlockSpec((tm, RET_QK_DIM // 2), lambda bi, t: (t, 0))
    sds = lambda width, dt: jax.ShapeDtypeStruct((b, t_len, width), dt)
    return pl.pallas_call(
        functools.partial(_odd_in_kernel, pad),
        grid=(b, t_len // tm),
        in_specs=[tok(D_MODEL), _const_spec((1, D_MODEL)), _const_spec(w_in.shape), tab, tab],
        out_specs=[tok(dq), tok(dq), tok(dv), tok(dv)],
        out_shape=[sds(dq, BF16), sds(dq, BF16), sds(dv, BF16), sds(dv, F32)],
        compiler_params=_params(2),
        name="odd_in",
    )(x3, g, w_in, cos, sin)


def _ret_prompt_kernel(lg_ref, q_ref, k_ref, v_ref, sg_ref, o_ref, st_ref, state_ref):
    c = pl.program_id(1)

    @pl.when(c == 0)
    def _():
        state_ref[...] = jnp.zeros_like(state_ref)

    row = lax.broadcasted_iota(jnp.int32, (BLK, BLK), 0)
    colm = lax.broadcasted_iota(jnp.int32, (BLK, BLK), 1)
    diff = (row - colm).astype(F32)
    idx = lax.broadcasted_iota(jnp.int32, (BLK, 1), 0).astype(F32)
    for hd in range(RET_HEADS):
        lg = lg_ref[hd]
        decay = jnp.where(diff >= 0, jnp.exp(lg * jnp.maximum(diff, 0.0)), 0.0)
        q_dec = jnp.exp(lg * (idx + 1.0))
        k_dec = jnp.exp(lg * (BLK - 1.0 - idx))
        chunk_dec = jnp.exp(jnp.full((1, 1), BLK, F32) * lg)
        qk_cols = slice(hd * RET_QK_DIM, (hd + 1) * RET_QK_DIM)
        v_cols = slice(hd * RET_V_DIM, (hd + 1) * RET_V_DIM)
        q = q_ref[0, :, qk_cols]
        k = k_ref[0, :, qk_cols]
        v = v_ref[0, :, v_cols]
        state = state_ref[hd]
        scores = (_dot_nt(q, k) * decay).astype(BF16)
        o = _dot(scores, v) + _dot(q, state.astype(BF16)) * q_dec
        new_state = state * chunk_dec + _dot_tn((k.astype(F32) * k_dec).astype(BF16), v)
        state_ref[hd] = new_state
        on = o * lax.rsqrt(jnp.mean(o * o, axis=-1, keepdims=True) + EPS)
        o_ref[0, :, v_cols] = (sg_ref[0, :, v_cols] * on).astype(BF16)

    @pl.when(c == pl.num_programs(1) - 1)
    def _():
        st_ref[0] = state_ref[...]


def _ret_prompt(lg, q, k, v, sg):
    b = q.shape[0]
    qk_spec = pl.BlockSpec((1, BLK, RET_HEADS * RET_QK_DIM), lambda bi, c: (bi, c, 0))
    v_spec = pl.BlockSpec((1, BLK, RET_HEADS * RET_V_DIM), lambda bi, c: (bi, c, 0))
    st_shape = (RET_HEADS, RET_QK_DIM, RET_V_DIM)
    return pl.pallas_call(
        _ret_prompt_kernel,
        grid=(b, NB),
        in_specs=[pl.BlockSpec(memory_space=pltpu.SMEM), qk_spec, qk_spec, v_spec, v_spec],
        out_specs=[v_spec, pl.BlockSpec((1,) + st_shape, lambda bi, c: (bi, 0, 0, 0))],
        out_shape=[jax.ShapeDtypeStruct((b, TP, RET_HEADS * RET_V_DIM), BF16),
                   jax.ShapeDtypeStruct((b,) + st_shape, F32)],
        scratch_shapes=[pltpu.VMEM(st_shape, F32)],
        compiler_params=_params(2),
        name="retention_prompt",
    )(lg, q, k, v, sg)


def _ret_decode_kernel(lg_ref, q_ref, k_ref, v_ref, sg_ref, s_ref, o_ref, st_ref):
    lg = lg_ref[pl.program_id(1)]
    gamma = jnp.exp(jnp.full((1, 1), 1.0, F32) * lg)
    q = q_ref[0]
    k = k_ref[0]
    v = v_ref[0]
    state = s_ref[0, 0]
    row0 = lax.broadcasted_iota(jnp.int32, (8, 1), 0) == 0
    qf = q.astype(F32)
    kf = k.astype(F32)
    q8 = jnp.where(row0, jnp.broadcast_to(qf, (8, RET_QK_DIM)), 0.0).astype(BF16)
    k8 = jnp.where(row0, jnp.broadcast_to(kf, (8, RET_QK_DIM)), 0.0).astype(BF16)
    v8 = jnp.broadcast_to(v.astype(F32), (8, RET_V_DIM)).astype(BF16)
    score = jnp.sum(qf * kf, axis=-1, keepdims=True).astype(BF16).astype(F32)
    cross = _dot(q8, state.astype(BF16))[0:1, :] * gamma
    o = score * v.astype(F32) + cross
    st_ref[0, 0] = state * gamma + _dot_tn(k8, v8)
    on = o * lax.rsqrt(jnp.mean(o * o, axis=-1, keepdims=True) + EPS)
    o_ref[0] = (sg_ref[0] * on).astype(BF16)


def _ret_decode(lg, q, k, v, sg, state):
    nb = q.shape[0]
    qk_spec = pl.BlockSpec((1, 1, RET_QK_DIM), lambda bi, hd: (bi, 0, hd))
    v_spec = pl.BlockSpec((1, 1, RET_V_DIM), lambda bi, hd: (bi, 0, hd))
    st_spec = pl.BlockSpec((1, 1, RET_QK_DIM, RET_V_DIM), lambda bi, hd: (bi, hd, 0, 0))
    return pl.pallas_call(
        _ret_decode_kernel,
        grid=(nb, RET_HEADS),
        in_specs=[pl.BlockSpec(memory_space=pltpu.SMEM), qk_spec, qk_spec, v_spec, v_spec, st_spec],
        out_specs=[v_spec, st_spec],
        out_shape=[jax.ShapeDtypeStruct((nb, 1, RET_HEADS * RET_V_DIM), BF16),
                   jax.ShapeDtypeStruct(state.shape, F32)],
        compiler_params=_params(2),
        name="retention_decode",
    )(lg, q, k, v, sg, state)


def _rope_tables(pos):
    half = RET_QK_DIM // 2
    inv = ROPE_BASE ** (-jnp.arange(half, dtype=F32) / half)
    ang = pos.astype(F32)[:, None] * inv[None, :]
    return jnp.cos(ang), jnp.sin(ang)


def kernel(x_prompt, x_sample, cache_k, cache_v, state_pool, state_ret, page_table, meta_tokens, norm_g,
           ffn_w_gate, ffn_w_up, ffn_w_down, even_w_in, even_w_out, pool_w, pool_scale, sb_bias,
           odd_w_in, odd_w_out):
    b = x_prompt.shape[0]
    db = x_sample.shape[0]
    depth = norm_g.shape[0]

    xp = jnp.concatenate([jnp.zeros((b, PAD, D_MODEL), F32),
                          jnp.broadcast_to(meta_tokens[None].astype(F32), (b, N_META, D_MODEL)),
                          x_prompt], axis=1).reshape(b * TP, D_MODEL)
    xs = x_sample.reshape(db, D_MODEL)

    lg = jnp.log1p(-jnp.exp2(-5.0 - jnp.arange(RET_HEADS, dtype=F32)))
    cos_p, sin_p = _rope_tables(jnp.arange(TP) - PAD)
    cos_s, sin_s = _rope_tables(jnp.full((db,), PAST_LEN))

    wg = ffn_w_gate.astype(BF16)
    wu = ffn_w_up.astype(BF16)
    wd = ffn_w_down.astype(BF16)

    def ffn(x, layer, half, tm):
        g2 = norm_g[layer, 4 * half:4 * half + 2]
        return _ffn(x, g2, wg[layer, half], wu[layer, half], wd[layer, half], tm)

    pool_p, pool_s, k_p, v_p, k_s, v_s, ret_p, ret_s = [], [], [], [], [], [], [], []
    for layer in range(depth):
        xp = ffn(xp, layer, 0, TM)
        xs = ffn(xs, layer, 0, db)
        g_in = norm_g[layer, 2:3]
        g_out = norm_g[layer, 3:4]
        if layer % 2 == 0:
            e = layer // 2
            w_in = even_w_in[e].astype(BF16)
            w_out = even_w_out[e].astype(BF16)
            pw = pool_w[e].astype(BF16)
            ps = pool_scale[e][None]
            pool_o, q, kb, vb, kf, vf, hist = _even_in(xp.reshape(b, TP, D_MODEL), g_in, w_in, pw, ps)
            sb_o = _sb_prompt(q, kb, vb, sb_bias[e])
            xp = _mix_out(xp, g_out, [pool_o.reshape(b * TP, POOL_WIDTH), sb_o.reshape(b * TP, SB_WIDTH)],
                          [w_out[:POOL_WIDTH], w_out[POOL_WIDTH:]], TM)
            pool_p.append(hist[:, HIST_ROWS - POOL_HIST:])
            k_p.append(kf[:, PAD:].reshape(b, T_REAL, SB_HEADS, SB_HEAD_DIM))
            v_p.append(vf[:, PAD:].reshape(b, T_REAL, SB_HEADS, SB_HEAD_DIM))

            hist_s = state_pool[e]
            pool_os, qs, ks, vs, pin_s = _even_in_dec(xs, g_in, w_in, pw, ps,
                                                     hist_s.reshape(db, POOL_HIST * POOL_WIDTH))
            sb_os = _sb_decode(page_table, qs.reshape(db, SB_HEADS, SB_HEAD_DIM),
                               jnp.broadcast_to(sb_bias[e][:, None], (SB_HEADS, PAGE_COLS)),
                               cache_k, cache_v, e)
            xs = _mix_out(xs, g_out, [pool_os, sb_os.reshape(db, SB_WIDTH)],
                          [w_out[:POOL_WIDTH], w_out[POOL_WIDTH:]], db)
            pool_s.append(jnp.concatenate([hist_s[:, 1:], pin_s[:, None]], axis=1))
            k_s.append(ks.reshape(db, 1, SB_HEADS, SB_HEAD_DIM))
            v_s.append(vs.reshape(db, 1, SB_HEADS, SB_HEAD_DIM))
        else:
            o = layer // 2
            w_in = odd_w_in[o].astype(BF16)
            w_out = odd_w_out[o].astype(BF16)
            q, k, v, sg = _odd_in(xp.reshape(b, TP, D_MODEL), g_in, w_in, cos_p, sin_p, TM, PAD)
            gated, st = _ret_prompt(lg, q, k, v, sg)
            xp = _mix_out(xp, g_out, [gated.reshape(b * TP, RET_HEADS * RET_V_DIM)], [w_out], TM)
            ret_p.append(st)

            qs, ks, vs, sgs = _odd_in(xs.reshape(1, db, D_MODEL), g_in, w_in, cos_s, sin_s, db, 0)
            to_rows = lambda a: a.reshape(db, 1, a.shape[-1])
            gated_s, st_s = _ret_decode(lg, to_rows(qs), to_rows(ks), to_rows(vs), to_rows(sgs), state_ret[o])
            xs = _mix_out(xs, g_out, [gated_s.reshape(db, RET_HEADS * RET_V_DIM)], [w_out], db)
            ret_s.append(st_s)
        xp = ffn(xp, layer, 1, TM)
        xs = ffn(xs, layer, 1, db)

    y_prompt = xp.reshape(b, TP, D_MODEL)[:, TP - SEQ:]
    y_sample = xs.reshape(db, 1, D_MODEL)
    return (y_prompt, y_sample, jnp.stack(pool_p), jnp.stack(pool_s), jnp.stack(k_p), jnp.stack(v_p),
            jnp.stack(k_s), jnp.stack(v_s), jnp.stack(ret_p), jnp.stack(ret_s))
```

```python
import functools
import math

import jax
import jax.numpy as jnp
from jax import lax
from jax.experimental import pallas as pl
from jax.experimental.pallas import tpu as pltpu

F32 = jnp.float32
BF16 = jnp.bfloat16

D_MODEL = 1024
SEQ = 8192
PAST_LEN = 16384
PAGE_SIZE = 128
N_META = 16
POOL_WIDTH = D_MODEL // 2
POOL_WINDOWS = (2, 4, 8, 16)
POOL_GROUP = POOL_WIDTH // len(POOL_WINDOWS)
POOL_HIST = max(POOL_WINDOWS) - 1
SB_HEADS = 8
SB_HEAD_DIM = (D_MODEL // 2) // SB_HEADS
SB_WIDTH = SB_HEADS * SB_HEAD_DIM
RET_HEADS = 4
RET_QK_DIM = D_MODEL // RET_HEADS
RET_V_DIM = 2 * D_MODEL // RET_HEADS
ROPE_BASE = 10000.0
D_FF = 2816
EPS = 1e-6

BLK = 128
T_REAL = N_META + SEQ
NB = -(-T_REAL // BLK)
TP = NB * BLK
PAD = TP - T_REAL
HIST_ROWS = 16
TM = 640
FF_CHUNK = 256
BLOCKS_PER_GROUP = 13
Q_GROUP = BLOCKS_PER_GROUP * BLK
PAD_LOGIT = 1e30
VMEM_LIMIT = 56 * 1024 * 1024


def _params(n_grid):
    return pltpu.CompilerParams(dimension_semantics=("arbitrary",) * n_grid,
                                vmem_limit_bytes=VMEM_LIMIT)


def _const_spec(shape):
    nd = len(shape)
    return pl.BlockSpec(shape, lambda *_: (0,) * nd, pipeline_mode=pl.Buffered(1))


def _rms(x, g):
    return x * lax.rsqrt(jnp.mean(x * x, axis=-1, keepdims=True) + EPS) * g


def _dot(a, b):
    return jnp.dot(a, b, preferred_element_type=F32)


def _dot_nt(a, b):
    return lax.dot_general(a, b, (((1,), (1,)), ((), ())), preferred_element_type=F32)


def _dot_tn(a, b):
    return lax.dot_general(a, b, (((0,), (0,)), ((), ())), preferred_element_type=F32)


def _neg_softplus(z):
    return -(jnp.maximum(z, 0.0) + jnp.log(1.0 + jnp.exp(-jnp.abs(z))))


def _cumsum_matrix():
    r = lax.broadcasted_iota(jnp.int32, (2 * BLK, 2 * BLK), 0) & (BLK - 1)
    c = lax.broadcasted_iota(jnp.int32, (2 * BLK, 2 * BLK), 1)
    return jnp.where((c >= BLK) | (r >= c), 1.0, 0.0).astype(BF16)


def _suffix_sums(lm, carry, m2s):
    hi = lm.astype(BF16)
    lo = (lm - hi.astype(F32)).astype(BF16)
    res = _dot(jnp.concatenate([hi, lo], axis=1), m2s) + jnp.concatenate([carry, carry], axis=1)
    return res[:, :BLK], res[:, BLK:]


def _ffn_kernel(x_ref, g_ref, wg_ref, wu_ref, wd_ref, o_ref, act_ref):
    x = x_ref[...]
    h = _rms(x, g_ref[0:1, :]).astype(BF16)
    for c in range(D_FF // FF_CHUNK):
        sl = slice(c * FF_CHUNK, (c + 1) * FF_CHUNK)
        gate = _dot(h, wg_ref[:, sl])
        up = _dot(h, wu_ref[:, sl])
        act_ref[:, sl] = (gate * jax.nn.sigmoid(gate) * up).astype(BF16)
    f = _dot(act_ref[...], wd_ref[...])
    o_ref[...] = x + 0.5 * _rms(f, g_ref[1:2, :])


def _ffn(x, g2, wg, wu, wd, tm):
    n = x.shape[0]
    return pl.pallas_call(
        _ffn_kernel,
        grid=(n // tm,),
        in_specs=[pl.BlockSpec((tm, D_MODEL), lambda i: (i, 0)),
                  _const_spec((2, D_MODEL)),
                  _const_spec((D_MODEL, D_FF)),
                  _const_spec((D_MODEL, D_FF)),
                  _const_spec((D_FF, D_MODEL))],
        out_specs=pl.BlockSpec((tm, D_MODEL), lambda i: (i, 0)),
        out_shape=jax.ShapeDtypeStruct((n, D_MODEL), F32),
        scratch_shapes=[pltpu.VMEM((tm, D_FF), BF16)],
        compiler_params=_params(1),
        name="ffn_half",
    )(x, g2, wg, wu, wd)


def _mix_out_kernel(n_parts, x_ref, g_ref, *refs):
    parts = refs[:n_parts]
    ws = refs[n_parts:2 * n_parts]
    o_ref = refs[2 * n_parts]
    y = _dot(parts[0][...], ws[0][...])
    for p, w in zip(parts[1:], ws[1:]):
        y = y + _dot(p[...], w[...])
    o_ref[...] = x_ref[...] + _rms(y, g_ref[...])


def _mix_out(x, g, parts, ws, tm):
    n = x.shape[0]
    in_specs = [pl.BlockSpec((tm, D_MODEL), lambda i: (i, 0)), _const_spec((1, D_MODEL))]
    in_specs += [pl.BlockSpec((tm, p.shape[1]), lambda i: (i, 0)) for p in parts]
    in_specs += [_const_spec(w.shape) for w in ws]
    return pl.pallas_call(
        functools.partial(_mix_out_kernel, len(parts)),
        grid=(n // tm,),
        in_specs=in_specs,
        out_specs=pl.BlockSpec((tm, D_MODEL), lambda i: (i, 0)),
        out_shape=jax.ShapeDtypeStruct((n, D_MODEL), F32),
        compiler_params=_params(1),
        name="mix_out",
    )(x, g, *parts, *ws)


def _pool_mix(sums, pool_in, inv_cnt, pw_ref, ps_ref):
    outs = []
    for g in range(len(POOL_WINDOWS)):
        cols = slice(g * POOL_GROUP, (g + 1) * POOL_GROUP)
        pooled = sums[g] * inv_cnt[g] - pool_in[:, cols]
        outs.append(_dot(pooled.astype(BF16), pw_ref[g]) * ps_ref[:, cols])
    return jnp.concatenate(outs, axis=1)


def _even_in_kernel(x_ref, g_ref, w_ref, pw_ref, ps_ref,
                    pool_ref, q_ref, kb_ref, vb_ref, kf_ref, vf_ref, hist_ref, ext_ref):
    t = pl.program_id(1)
    tm = x_ref.shape[1]
    h = _rms(x_ref[0], g_ref[...]).astype(BF16)
    pos = t * tm + lax.broadcasted_iota(jnp.int32, (tm, 1), 0) - PAD
    real = pos >= 0

    pool_in = jnp.where(real, _dot(h, w_ref[:, 0:POOL_WIDTH]), 0.0)
    q = _dot(h, w_ref[:, POOL_WIDTH:POOL_WIDTH + SB_WIDTH])
    k = _dot(h, w_ref[:, POOL_WIDTH + SB_WIDTH:POOL_WIDTH + 2 * SB_WIDTH])
    v = _dot(h, w_ref[:, POOL_WIDTH + 2 * SB_WIDTH:POOL_WIDTH + 3 * SB_WIDTH])
    q_ref[0] = (q * -(SB_HEAD_DIM ** -0.5)).astype(BF16)
    kf_ref[0] = k
    vf_ref[0] = v
    kb_ref[0] = k.astype(BF16)
    vb_ref[0] = v.astype(BF16)

    @pl.when(t == 0)
    def _():
        ext_ref[0:HIST_ROWS, :] = jnp.zeros((HIST_ROWS, POOL_WIDTH), F32)

    ext_ref[HIST_ROWS:, :] = pool_in
    sums, inv_cnt = [], []
    for g, w in enumerate(POOL_WINDOWS):
        cols = slice(g * POOL_GROUP, (g + 1) * POOL_GROUP)
        s = pool_in[:, cols]
        for i in range(1, w):
            s = s + ext_ref[HIST_ROWS - i:HIST_ROWS - i + tm, cols]
        sums.append(s)
        inv_cnt.append(1.0 / jnp.clip(pos + 1, 1, w).astype(F32))
    pool_ref[0] = _pool_mix(sums, pool_in, inv_cnt, pw_ref, ps_ref).astype(BF16)
    tail = ext_ref[tm:tm + HIST_ROWS, :]
    ext_ref[0:HIST_ROWS, :] = tail
    hist_ref[0] = tail


def _even_in(x3, g, w_in, pool_w, pool_scale):
    b = x3.shape[0]
    d_in = w_in.shape[1]
    tok = lambda width: pl.BlockSpec((1, TM, width), lambda bi, t: (bi, t, 0))
    sds = lambda width, dt: jax.ShapeDtypeStruct((b, TP, width), dt)
    return pl.pallas_call(
        _even_in_kernel,
        grid=(b, TP // TM),
        in_specs=[tok(D_MODEL), _const_spec((1, D_MODEL)), _const_spec((D_MODEL, d_in)),
                  _const_spec(pool_w.shape), _const_spec((1, POOL_WIDTH))],
        out_specs=[tok(POOL_WIDTH), tok(SB_WIDTH), tok(SB_WIDTH), tok(SB_WIDTH), tok(SB_WIDTH), tok(SB_WIDTH),
                   pl.BlockSpec((1, HIST_ROWS, POOL_WIDTH), lambda bi, t: (bi, 0, 0))],
        out_shape=[sds(POOL_WIDTH, BF16), sds(SB_WIDTH, BF16), sds(SB_WIDTH, BF16), sds(SB_WIDTH, BF16),
                   sds(SB_WIDTH, F32), sds(SB_WIDTH, F32),
                   jax.ShapeDtypeStruct((b, HIST_ROWS, POOL_WIDTH), F32)],
        scratch_shapes=[pltpu.VMEM((TM + HIST_ROWS, POOL_WIDTH), F32)],
        compiler_params=_params(2),
        name="even_in_prompt",
    )(x3, g, w_in, pool_w, pool_scale)


def _even_in_dec_kernel(x_ref, g_ref, w_ref, pw_ref, ps_ref, hist_ref,
                        pool_ref, q_ref, k_ref, v_ref, pin_ref):
    h = _rms(x_ref[...], g_ref[...]).astype(BF16)
    pool_in = _dot(h, w_ref[:, 0:POOL_WIDTH])
    q = _dot(h, w_ref[:, POOL_WIDTH:POOL_WIDTH + SB_WIDTH])
    k_ref[...] = _dot(h, w_ref[:, POOL_WIDTH + SB_WIDTH:POOL_WIDTH + 2 * SB_WIDTH])
    v_ref[...] = _dot(h, w_ref[:, POOL_WIDTH + 2 * SB_WIDTH:POOL_WIDTH + 3 * SB_WIDTH])
    q_ref[...] = (q * (SB_HEAD_DIM ** -0.5)).astype(BF16)
    pin_ref[...] = pool_in
    sums, inv_cnt = [], []
    for g, w in enumerate(POOL_WINDOWS):
        s = pool_in[:, g * POOL_GROUP:(g + 1) * POOL_GROUP]
        for i in range(1, w):
            base = (POOL_HIST - i) * POOL_WIDTH + g * POOL_GROUP
            s = s + hist_ref[:, base:base + POOL_GROUP]
        sums.append(s)
        inv_cnt.append(1.0 / w)
    pool_ref[...] = _pool_mix(sums, pool_in, inv_cnt, pw_ref, ps_ref).astype(BF16)


def _even_in_dec(x, g, w_in, pool_w, pool_scale, hist2d):
    n = x.shape[0]
    d_in = w_in.shape[1]
    full = lambda shape: pl.BlockSpec(shape, lambda i: (0,) * len(shape))
    return pl.pallas_call(
        _even_in_dec_kernel,
        grid=(1,),
        in_specs=[full((n, D_MODEL)), full((1, D_MODEL)), full((D_MODEL, d_in)), full(pool_w.shape),
                  full((1, POOL_WIDTH)), full(hist2d.shape)],
        out_specs=[full((n, POOL_WIDTH)), full((n, SB_WIDTH)), full((n, SB_WIDTH)), full((n, SB_WIDTH)),
                   full((n, POOL_WIDTH))],
        out_shape=[jax.ShapeDtypeStruct((n, POOL_WIDTH), BF16), jax.ShapeDtypeStruct((n, SB_WIDTH), BF16),
                   jax.ShapeDtypeStruct((n, SB_WIDTH), F32), jax.ShapeDtypeStruct((n, SB_WIDTH), F32),
                   jax.ShapeDtypeStruct((n, POOL_WIDTH), F32)],
        compiler_params=_params(1),
        name="even_in_decode",
    )(x, g, w_in, pool_w, pool_scale, hist2d)


def _sb_prompt_kernel(bias_ref, q_ref, k_ref, v_ref, o_ref, acc_ref, carry_ref):
    p = pl.program_id(1)
    acc_ref[...] = jnp.zeros_like(acc_ref)
    carry_ref[...] = jnp.zeros_like(carry_ref)
    m2s = _cumsum_matrix()
    lane = lax.broadcasted_iota(jnp.int32, (1, BLK), 1)
    first = lane < SB_HEAD_DIM
    lane2 = lax.broadcasted_iota(jnp.int32, (1, 2 * BLK), 1)
    neg_bias = jnp.where(lane2 < BLK, -bias_ref[2 * p], -bias_ref[2 * p + 1])
    key_in_blk = lane2 & (BLK - 1)
    row_in_grp = lax.broadcasted_iota(jnp.int32, (Q_GROUP, 1), 0)

    def kv_block(j):
        rows = pl.ds(pl.multiple_of(j * BLK, BLK), BLK)
        kj = k_ref[0, rows, :].astype(F32)
        vj = v_ref[0, rows, :].astype(F32)
        kcat = jnp.concatenate([jnp.where(first, kj, 0.0), jnp.where(first, 0.0, kj)], axis=0)
        vcat = jnp.concatenate([jnp.where(first, vj, 0.0), jnp.where(first, 0.0, vj)], axis=0)
        return kcat.astype(BF16), vcat.astype(BF16)

    def q_group(g, j, kcat, vcat, nb_j, masked):
        rows = pl.ds(pl.multiple_of(g * Q_GROUP, Q_GROUP), Q_GROUP)
        u = _dot_nt(q_ref[0, rows, :], kcat) + nb_j
        lm = jnp.minimum(u, 0.0) - jnp.log(1.0 + jnp.exp(-jnp.abs(u)))
        if masked:
            valid = (j * BLK + key_in_blk) < (g * Q_GROUP + row_in_grp)
            lm = jnp.where(valid, lm, 0.0)
        cs = []
        for hh in range(2):
            cols = slice(hh * BLK, (hh + 1) * BLK)
            c, carry_ref[rows, cols] = _suffix_sums(lm[:, cols], carry_ref[rows, cols], m2s)
            cs.append(c)
        w = jnp.exp(jnp.concatenate(cs, axis=1) - u)
        if masked:
            w = jnp.where(valid, w, 0.0)
        acc_ref[rows, :] += _dot(w.astype(BF16), vcat)

    def j_body(jj, _):
        j = NB - 1 - jj
        kcat, vcat = kv_block(j)
        nb_j = neg_bias + jnp.where(j * BLK + key_in_blk < PAD, PAD_LOGIT, 0.0)
        g0 = j // BLOCKS_PER_GROUP
        q_group(g0, j, kcat, vcat, nb_j, True)

        def g_body(g, _):
            q_group(g, j, kcat, vcat, nb_j, False)
            return 0

        lax.fori_loop(g0 + 1, TP // Q_GROUP, g_body, 0)
        return 0

    lax.fori_loop(0, NB, j_body, 0)
    o_ref[0] = acc_ref[...].astype(BF16)


def _sb_prompt(q, k, v, bias):
    b = q.shape[0]
    spec = pl.BlockSpec((1, TP, BLK), lambda bi, p: (bi, 0, p))
    return pl.pallas_call(
        _sb_prompt_kernel,
        grid=(b, SB_WIDTH // BLK),
        in_specs=[pl.BlockSpec(memory_space=pltpu.SMEM), spec, spec, spec],
        out_specs=spec,
        out_shape=jax.ShapeDtypeStruct((b, TP, SB_WIDTH), BF16),
        scratch_shapes=[pltpu.VMEM((TP, BLK), F32), pltpu.VMEM((TP, 2 * BLK), F32)],
        compiler_params=_params(2),
        name="sb_prompt",
    )(bias, q, k, v)


PAGES_PER_STEP = 8


def _sb_decode_kernel(pt_ref, q_ref, bias_ref, *refs):
    k_refs = refs[:PAGES_PER_STEP]
    v_refs = refs[PAGES_PER_STEP:2 * PAGES_PER_STEP]
    o_ref, acc_ref, carry_ref = refs[2 * PAGES_PER_STEP:]
    i = pl.program_id(1)

    @pl.when(i == 0)
    def _():
        acc_ref[...] = jnp.zeros_like(acc_ref)
        carry_ref[...] = jnp.zeros_like(carry_ref)

    head = lax.broadcasted_iota(jnp.int32, (SB_HEADS, SB_WIDTH), 0)
    col = lax.broadcasted_iota(jnp.int32, (SB_HEADS, SB_WIDTH), 1)
    own = (col // SB_HEAD_DIM) == head
    qbd = jnp.where(own, jnp.broadcast_to(q_ref[0].astype(F32), (SB_HEADS, SB_WIDTH)), 0.0).astype(BF16)
    m2s = _cumsum_matrix()
    bias = bias_ref[...]
    zs = [_dot(qbd, k_refs[r][...].reshape(SB_WIDTH, PAGE_SIZE).astype(BF16)) + bias
          for r in range(PAGES_PER_STEP)]
    stacked = jnp.concatenate([_neg_softplus(z) for z in zs], axis=0)
    hi = stacked.astype(BF16)
    lo = (stacked - hi.astype(F32)).astype(BF16)
    res = _dot(jnp.concatenate([hi, lo], axis=1), m2s)
    offset = carry_ref[...]
    acc = acc_ref[...]
    for r in range(PAGES_PER_STEP):
        rows = slice(r * SB_HEADS, (r + 1) * SB_HEADS)
        w = jnp.exp(zs[r] + res[rows, :BLK] + offset)
        offset = offset + res[rows, BLK:]
        vt = v_refs[r][...].reshape(SB_WIDTH, PAGE_SIZE).astype(BF16)
        acc = acc + _dot_nt(w.astype(BF16), vt)
    carry_ref[...] = offset
    acc_ref[...] = acc

    @pl.when(i == pl.num_programs(1) - 1)
    def _():
        o_ref[0] = jnp.sum(jnp.where(own, acc, 0.0), axis=0, keepdims=True).astype(BF16)


def _sb_decode(page_table, q3, bias_b, cache_kt, cache_vt, e):
    nb, n_pages = page_table.shape
    steps = n_pages // PAGES_PER_STEP

    def page_spec(r):
        return pl.BlockSpec((None, None, SB_HEADS, SB_HEAD_DIM, PAGE_SIZE),
                            lambda b, i, pt: (e, pt[b, n_pages - 1 - (i * PAGES_PER_STEP + r)], 0, 0, 0))

    grid_spec = pltpu.PrefetchScalarGridSpec(
        num_scalar_prefetch=1,
        grid=(nb, steps),
        in_specs=[pl.BlockSpec((1, 1, SB_WIDTH), lambda b, i, pt: (b, 0, 0)),
                  pl.BlockSpec((SB_HEADS, BLK), lambda b, i, pt: (0, 0))]
                 + [page_spec(r) for r in range(PAGES_PER_STEP)] * 2,
        out_specs=pl.BlockSpec((1, 1, SB_WIDTH), lambda b, i, pt: (b, 0, 0)),
        scratch_shapes=[pltpu.VMEM((SB_HEADS, SB_WIDTH), F32), pltpu.VMEM((SB_HEADS, BLK), F32)],
    )
    return pl.pallas_call(
        _sb_decode_kernel,
        grid_spec=grid_spec,
        out_shape=jax.ShapeDtypeStruct((nb, 1, SB_WIDTH), BF16),
        compiler_params=_params(2),
        name="sb_decode",
    )(page_table, q3, bias_b, *([cache_kt] * PAGES_PER_STEP), *([cache_vt] * PAGES_PER_STEP))


def _odd_in_kernel(pad, x_ref, g_ref, w_ref, cos_ref, sin_ref, q_ref, k_ref, v_ref, sg_ref):
    t = pl.program_id(1)
    tm = x_ref.shape[1]
    h = _rms(x_ref[0], g_ref[...]).astype(BF16)
    dq = RET_HEADS * RET_QK_DIM
    dv = RET_HEADS * RET_V_DIM
    cos = cos_ref[...]
    sin = sin_ref[...]
    half = RET_QK_DIM // 2
    if pad:
        real = (t * tm + lax.broadcasted_iota(jnp.int32, (tm, 1), 0)) >= pad
    for hd in range(RET_HEADS):
        for which, out_ref, scale in ((0, q_ref, 1.0), (1, k_ref, RET_QK_DIM ** -0.5)):
            base = which * dq + hd * RET_QK_DIM
            a = _dot(h, w_ref[:, base:base + RET_QK_DIM])
            x1, x2 = a[:, :half], a[:, half:]
            rot = jnp.concatenate([x1 * cos - x2 * sin, x1 * sin + x2 * cos], axis=1) * scale
            if pad and which == 1:
                rot = jnp.where(real, rot, 0.0)
            out_ref[0, :, hd * RET_QK_DIM:(hd + 1) * RET_QK_DIM] = rot.astype(BF16)
        cols = slice(hd * RET_V_DIM, (hd + 1) * RET_V_DIM)
        v = _dot(h, w_ref[:, 2 * dq + hd * RET_V_DIM:2 * dq + (hd + 1) * RET_V_DIM])
        if pad:
            v = jnp.where(real, v, 0.0)
        v_ref[0, :, cols] = v.astype(BF16)
        gt = _dot(h, w_ref[:, 2 * dq + dv + hd * RET_V_DIM:2 * dq + dv + (hd + 1) * RET_V_DIM])
        sg_ref[0, :, cols] = gt * jax.nn.sigmoid(gt)


def _odd_in(x3, g, w_in, cos, sin, tm, pad):
    b, t_len, _ = x3.shape
    dq = RET_HEADS * RET_QK_DIM
    dv = RET_HEADS * RET_V_DIM
    tok = lambda width: pl.BlockSpec((1, tm, width), lambda bi, t: (bi, t, 0))
    tab = pl.BlockSpec((tm, RET_QK_DIM // 2), lambda bi, t: (t, 0))
    sds = lambda width, dt: jax.ShapeDtypeStruct((b, t_len, width), dt)
    return pl.pallas_call(
        functools.partial(_odd_in_kernel, pad),
        grid=(b, t_len // tm),
        in_specs=[tok(D_MODEL), _const_spec((1, D_MODEL)), _const_spec(w_in.shape), tab, tab],
        out_specs=[tok(dq), tok(dq), tok(dv), tok(dv)],
        out_shape=[sds(dq, BF16), sds(dq, BF16), sds(dv, BF16), sds(dv, F32)],
        compiler_params=_params(2),
        name="odd_in",
    )(x3, g, w_in, cos, sin)


def _ret_prompt_kernel(lg_ref, q_ref, k_ref, v_ref, sg_ref, o_ref, st_ref, state_ref):
    c = pl.program_id(1)

    @pl.when(c == 0)
    def _():
        state_ref[...] = jnp.zeros_like(state_ref)

    row = lax.broadcasted_iota(jnp.int32, (BLK, BLK), 0)
    colm = lax.broadcasted_iota(jnp.int32, (BLK, BLK), 1)
    diff = (row - colm).astype(F32)
    idx = lax.broadcasted_iota(jnp.int32, (BLK, 1), 0).astype(F32)
    for hd in range(RET_HEADS):
        lg = lg_ref[hd]
        decay = jnp.where(diff >= 0, jnp.exp(lg * jnp.maximum(diff, 0.0)), 0.0)
        q_dec = jnp.exp(lg * (idx + 1.0))
        k_dec = jnp.exp(lg * (BLK - 1.0 - idx))
        chunk_dec = jnp.exp(jnp.full((1, 1), BLK, F32) * lg)
        qk_cols = slice(hd * RET_QK_DIM, (hd + 1) * RET_QK_DIM)
        v_cols = slice(hd * RET_V_DIM, (hd + 1) * RET_V_DIM)
        q = q_ref[0, :, qk_cols]
        k = k_ref[0, :, qk_cols]
        v = v_ref[0, :, v_cols]
        state = state_ref[hd]
        scores = (_dot_nt(q, k) * decay).astype(BF16)
        o = _dot(scores, v) + _dot(q, state.astype(BF16)) * q_dec
        new_state = state * chunk_dec + _dot_tn((k.astype(F32) * k_dec).astype(BF16), v)
        state_ref[hd] = new_state
        on = o * lax.rsqrt(jnp.mean(o * o, axis=-1, keepdims=True) + EPS)
        o_ref[0, :, v_cols] = (sg_ref[0, :, v_cols] * on).astype(BF16)

    @pl.when(c == pl.num_programs(1) - 1)
    def _():
        st_ref[0] = state_ref[...]


def _ret_prompt(lg, q, k, v, sg):
    b = q.shape[0]
    qk_spec = pl.BlockSpec((1, BLK, RET_HEADS * RET_QK_DIM), lambda bi, c: (bi, c, 0))
    v_spec = pl.BlockSpec((1, BLK, RET_HEADS * RET_V_DIM), lambda bi, c: (bi, c, 0))
    st_shape = (RET_HEADS, RET_QK_DIM, RET_V_DIM)
    return pl.pallas_call(
        _ret_prompt_kernel,
        grid=(b, NB),
        in_specs=[pl.BlockSpec(memory_space=pltpu.SMEM), qk_spec, qk_spec, v_spec, v_spec],
        out_specs=[v_spec, pl.BlockSpec((1,) + st_shape, lambda bi, c: (bi, 0, 0, 0))],
        out_shape=[jax.ShapeDtypeStruct((b, TP, RET_HEADS * RET_V_DIM), BF16),
                   jax.ShapeDtypeStruct((b,) + st_shape, F32)],
        scratch_shapes=[pltpu.VMEM(st_shape, F32)],
        compiler_params=_params(2),
        name="retention_prompt",
    )(lg, q, k, v, sg)


def _ret_decode_kernel(lg_ref, q_ref, k_ref, v_ref, sg_ref, s_ref, o_ref, st_ref):
    row0 = lax.broadcasted_iota(jnp.int32, (8, 1), 0) == 0
    for hd in range(RET_HEADS):
        lg = lg_ref[hd]
        gamma = jnp.exp(jnp.full((1, 1), 1.0, F32) * lg)
        qk_cols = slice(hd * RET_QK_DIM, (hd + 1) * RET_QK_DIM)
        v_cols = slice(hd * RET_V_DIM, (hd + 1) * RET_V_DIM)
        qf = q_ref[0, :, qk_cols].astype(F32)
        kf = k_ref[0, :, qk_cols].astype(F32)
        vf = v_ref[0, :, v_cols].astype(F32)
        state = s_ref[0, hd]
        q8 = jnp.where(row0, jnp.broadcast_to(qf, (8, RET_QK_DIM)), 0.0).astype(BF16)
        k8 = jnp.where(row0, jnp.broadcast_to(kf, (8, RET_QK_DIM)), 0.0).astype(BF16)
        v8 = jnp.broadcast_to(vf, (8, RET_V_DIM)).astype(BF16)
        score = jnp.sum(qf * kf, axis=-1, keepdims=True).astype(BF16).astype(F32)
        cross = _dot(q8, state.astype(BF16))[0:1, :] * gamma
        o = score * vf + cross
        st_ref[0, hd] = state * gamma + _dot_tn(k8, v8)
        on = o * lax.rsqrt(jnp.mean(o * o, axis=-1, keepdims=True) + EPS)
        o_ref[0, :, v_cols] = (sg_ref[0, :, v_cols] * on).astype(BF16)


def _ret_decode(lg, q, k, v, sg, state):
    nb = q.shape[0]
    qk_spec = pl.BlockSpec((1, 1, RET_HEADS * RET_QK_DIM), lambda bi: (bi, 0, 0))
    v_spec = pl.BlockSpec((1, 1, RET_HEADS * RET_V_DIM), lambda bi: (bi, 0, 0))
    st_spec = pl.BlockSpec((1, RET_HEADS, RET_QK_DIM, RET_V_DIM), lambda bi: (bi, 0, 0, 0))
    return pl.pallas_call(
        _ret_decode_kernel,
        grid=(nb,),
        in_specs=[pl.BlockSpec(memory_space=pltpu.SMEM), qk_spec, qk_spec, v_spec, v_spec, st_spec],
        out_specs=[v_spec, st_spec],
        out_shape=[jax.ShapeDtypeStruct((nb, 1, RET_HEADS * RET_V_DIM), BF16),
                   jax.ShapeDtypeStruct(state.shape, F32)],
        compiler_params=_params(1),
        name="retention_decode",
    )(lg, q, k, v, sg, state)


def _rope_tables(pos):
    half = RET_QK_DIM // 2
    inv = ROPE_BASE ** (-jnp.arange(half, dtype=F32) / half)
    ang = pos.astype(F32)[:, None] * inv[None, :]
    return jnp.cos(ang), jnp.sin(ang)


def kernel(x_prompt, x_sample, cache_k, cache_v, state_pool, state_ret, page_table, meta_tokens, norm_g,
           ffn_w_gate, ffn_w_up, ffn_w_down, even_w_in, even_w_out, pool_w, pool_scale, sb_bias,
           odd_w_in, odd_w_out):
    b = x_prompt.shape[0]
    db = x_sample.shape[0]
    depth = norm_g.shape[0]

    xp = jnp.concatenate([jnp.zeros((b, PAD, D_MODEL), F32),
                          jnp.broadcast_to(meta_tokens[None].astype(F32), (b, N_META, D_MODEL)),
                          x_prompt], axis=1).reshape(b * TP, D_MODEL)
    xs = x_sample.reshape(db, D_MODEL)

    lg = jnp.log1p(-jnp.exp2(-5.0 - jnp.arange(RET_HEADS, dtype=F32)))
    cos_p, sin_p = _rope_tables(jnp.arange(TP) - PAD)
    cos_s, sin_s = _rope_tables(jnp.full((db,), PAST_LEN))

    cache_kt = jnp.transpose(cache_k, (0, 1, 3, 4, 2))
    cache_vt = jnp.transpose(cache_v, (0, 1, 3, 4, 2))

    wg = ffn_w_gate.astype(BF16)
    wu = ffn_w_up.astype(BF16)
    wd = ffn_w_down.astype(BF16)

    def ffn(x, layer, half, tm):
        g2 = norm_g[layer, 4 * half:4 * half + 2]
        return _ffn(x, g2, wg[layer, half], wu[layer, half], wd[layer, half], tm)

    pool_p, pool_s, k_p, v_p, k_s, v_s, ret_p, ret_s = [], [], [], [], [], [], [], []
    for layer in range(depth):
        xp = ffn(xp, layer, 0, TM)
        xs = ffn(xs, layer, 0, db)
        g_in = norm_g[layer, 2:3]
        g_out = norm_g[layer, 3:4]
        if layer % 2 == 0:
            e = layer // 2
            w_in = even_w_in[e].astype(BF16)
            w_out = even_w_out[e].astype(BF16)
            pw = pool_w[e].astype(BF16)
            ps = pool_scale[e][None]
            pool_o, q, kb, vb, kf, vf, hist = _even_in(xp.reshape(b, TP, D_MODEL), g_in, w_in, pw, ps)
            sb_o = _sb_prompt(q, kb, vb, sb_bias[e])
            xp = _mix_out(xp, g_out, [pool_o.reshape(b * TP, POOL_WIDTH), sb_o.reshape(b * TP, SB_WIDTH)],
                          [w_out[:POOL_WIDTH], w_out[POOL_WIDTH:]], TM)
            pool_p.append(hist[:, HIST_ROWS - POOL_HIST:])
            k_p.append(kf[:, PAD:].reshape(b, T_REAL, SB_HEADS, SB_HEAD_DIM))
            v_p.append(vf[:, PAD:].reshape(b, T_REAL, SB_HEADS, SB_HEAD_DIM))

            hist_s = state_pool[e]
            pool_os, qs, ks, vs, pin_s = _even_in_dec(xs, g_in, w_in, pw, ps,
                                                     hist_s.reshape(db, POOL_HIST * POOL_WIDTH))
            sb_os = _sb_decode(page_table, qs.reshape(db, 1, SB_WIDTH),
                               jnp.broadcast_to(sb_bias[e][:, None], (SB_HEADS, BLK)),
                               cache_kt, cache_vt, e)
            xs = _mix_out(xs, g_out, [pool_os, sb_os.reshape(db, SB_WIDTH)],
                          [w_out[:POOL_WIDTH], w_out[POOL_WIDTH:]], db)
            pool_s.append(jnp.concatenate([hist_s[:, 1:], pin_s[:, None]], axis=1))
            k_s.append(ks.reshape(db, 1, SB_HEADS, SB_HEAD_DIM))
            v_s.append(vs.reshape(db, 1, SB_HEADS, SB_HEAD_DIM))
        else:
            o = layer // 2
            w_in = odd_w_in[o].astype(BF16)
            w_out = odd_w_out[o].astype(BF16)
            q, k, v, sg = _odd_in(xp.reshape(b, TP, D_MODEL), g_in, w_in, cos_p, sin_p, TM, PAD)
            gated, st = _ret_prompt(lg, q, k, v, sg)
            xp = _mix_out(xp, g_out, [gated.reshape(b * TP, RET_HEADS * RET_V_DIM)], [w_out], TM)
            ret_p.append(st)

            qs, ks, vs, sgs = _odd_in(xs.reshape(1, db, D_MODEL), g_in, w_in, cos_s, sin_s, db, 0)
            to_rows = lambda a: a.reshape(db, 1, a.shape[-1])
            gated_s, st_s = _ret_decode(lg, to_rows(qs), to_rows(ks), to_rows(vs), to_rows(sgs), state_ret[o])
            xs = _mix_out(xs, g_out, [gated_s.reshape(db, RET_HEADS * RET_V_DIM)], [w_out], db)
            ret_s.append(st_s)
        xp = ffn(xp, layer, 1, TM)
        xs = ffn(xs, layer, 1, db)

    y_prompt = xp.reshape(b, TP, D_MODEL)[:, TP - SEQ:]
    y_sample = xs.reshape(db, 1, D_MODEL)
    return (y_prompt, y_sample, jnp.stack(pool_p), jnp.stack(pool_s), jnp.stack(k_p), jnp.stack(v_p),
            jnp.stack(k_s), jnp.stack(v_s), jnp.stack(ret_p), jnp.stack(ret_s))
```

```python
import functools

import jax
import jax.numpy as jnp
from jax import lax
from jax.experimental import pallas as pl
from jax.experimental.pallas import tpu as pltpu

F32 = jnp.float32
BF16 = jnp.bfloat16

D_MODEL = 1024
SEQ = 8192
PAST_LEN = 16384
PAGE_SIZE = 128
N_META = 16
POOL_WIDTH = D_MODEL // 2
POOL_WINDOWS = (2, 4, 8, 16)
POOL_GROUP = POOL_WIDTH // len(POOL_WINDOWS)
POOL_HIST = max(POOL_WINDOWS) - 1
SB_HEADS = 8
SB_HEAD_DIM = (D_MODEL // 2) // SB_HEADS
SB_WIDTH = SB_HEADS * SB_HEAD_DIM
RET_HEADS = 4
RET_QK_DIM = D_MODEL // RET_HEADS
RET_V_DIM = 2 * D_MODEL // RET_HEADS
ROPE_BASE = 10000.0
D_FF = 2816
EPS = 1e-6

BLK = 128
T_REAL = N_META + SEQ
NB = -(-T_REAL // BLK)
TP = NB * BLK
PAD = TP - T_REAL
HIST_ROWS = 16
TM = 640
FF_CHUNK = 256
BLOCKS_PER_GROUP = 5
Q_GROUP = BLOCKS_PER_GROUP * BLK
HEAD_PAIRS = SB_WIDTH // BLK
PAD_LOGIT = 1e30
VMEM_LIMIT = 56 * 1024 * 1024


def _params(n_grid):
    return pltpu.CompilerParams(dimension_semantics=("arbitrary",) * n_grid,
                                vmem_limit_bytes=VMEM_LIMIT)


def _const_spec(shape):
    nd = len(shape)
    return pl.BlockSpec(shape, lambda *_: (0,) * nd, pipeline_mode=pl.Buffered(1))


def _rms(x, g):
    return x * lax.rsqrt(jnp.mean(x * x, axis=-1, keepdims=True) + EPS) * g


def _dot(a, b):
    return jnp.dot(a, b, preferred_element_type=F32)


def _dot_nt(a, b):
    return lax.dot_general(a, b, (((1,), (1,)), ((), ())), preferred_element_type=F32)


def _dot_tn(a, b):
    return lax.dot_general(a, b, (((0,), (0,)), ((), ())), preferred_element_type=F32)


def _neg_softplus(z):
    return -(jnp.maximum(z, 0.0) + jnp.log(1.0 + jnp.exp(-jnp.abs(z))))


def _cumsum_matrix():
    r = lax.broadcasted_iota(jnp.int32, (2 * BLK, 2 * BLK), 0) & (BLK - 1)
    c = lax.broadcasted_iota(jnp.int32, (2 * BLK, 2 * BLK), 1)
    return jnp.where((c >= BLK) | (r >= c), 1.0, 0.0).astype(BF16)


def _ffn_kernel(n_parts, x_ref, g_ref, wg_ref, wu_ref, wd_ref, *refs):
    o_ref, act_ref = refs[-2:]
    x = x_ref[...]
    if n_parts:
        gm_ref = refs[0]
        parts = refs[1:1 + n_parts]
        ws = refs[1 + n_parts:1 + 2 * n_parts]
        y = _dot(parts[0][...], ws[0][...])
        for p, w in zip(parts[1:], ws[1:]):
            y = y + _dot(p[...], w[...])
        x = x + _rms(y, gm_ref[...])
    h = _rms(x, g_ref[0:1, :]).astype(BF16)
    for c in range(D_FF // FF_CHUNK):
        sl = slice(c * FF_CHUNK, (c + 1) * FF_CHUNK)
        gate = _dot(h, wg_ref[:, sl])
        up = _dot(h, wu_ref[:, sl])
        act_ref[:, sl] = (gate * jax.nn.sigmoid(gate) * up).astype(BF16)
    f = _dot(act_ref[...], wd_ref[...])
    o_ref[...] = x + 0.5 * _rms(f, g_ref[1:2, :])


def _ffn(x, g2, wg, wu, wd, tm, g_mix=None, parts=(), ws=()):
    n = x.shape[0]
    in_specs = [pl.BlockSpec((tm, D_MODEL), lambda i: (i, 0)),
                _const_spec((2, D_MODEL)),
                _const_spec((D_MODEL, D_FF)),
                _const_spec((D_MODEL, D_FF)),
                _const_spec((D_FF, D_MODEL))]
    mix_args = []
    if parts:
        in_specs += [_const_spec((1, D_MODEL))]
        in_specs += [pl.BlockSpec((tm, p.shape[1]), lambda i: (i, 0)) for p in parts]
        in_specs += [_const_spec(w.shape) for w in ws]
        mix_args = [g_mix, *parts, *ws]
    return pl.pallas_call(
        functools.partial(_ffn_kernel, len(parts)),
        grid=(n // tm,),
        in_specs=in_specs,
        out_specs=pl.BlockSpec((tm, D_MODEL), lambda i: (i, 0)),
        out_shape=jax.ShapeDtypeStruct((n, D_MODEL), F32),
        scratch_shapes=[pltpu.VMEM((tm, D_FF), BF16)],
        compiler_params=_params(1),
        name="ffn_half",
    )(x, g2, wg, wu, wd, *mix_args)


def _pool_mix(sums, pool_in, inv_cnt, pw_ref, ps_ref):
    outs = []
    for g in range(len(POOL_WINDOWS)):
        cols = slice(g * POOL_GROUP, (g + 1) * POOL_GROUP)
        pooled = sums[g] * inv_cnt[g] - pool_in[:, cols]
        outs.append(_dot(pooled.astype(BF16), pw_ref[g]) * ps_ref[:, cols])
    return jnp.concatenate(outs, axis=1)


def _even_in_kernel(x_ref, g_ref, w_ref, pw_ref, ps_ref,
                    pool_ref, q_ref, kb_ref, vb_ref, kf_ref, vf_ref, hist_ref, ext_ref):
    t = pl.program_id(1)
    tm = x_ref.shape[1]
    h = _rms(x_ref[0], g_ref[...]).astype(BF16)
    pos = t * tm + lax.broadcasted_iota(jnp.int32, (tm, 1), 0) - PAD
    real = pos >= 0

    pool_in = jnp.where(real, _dot(h, w_ref[:, 0:POOL_WIDTH]), 0.0)
    q = _dot(h, w_ref[:, POOL_WIDTH:POOL_WIDTH + SB_WIDTH])
    k = _dot(h, w_ref[:, POOL_WIDTH + SB_WIDTH:POOL_WIDTH + 2 * SB_WIDTH])
    v = _dot(h, w_ref[:, POOL_WIDTH + 2 * SB_WIDTH:POOL_WIDTH + 3 * SB_WIDTH])
    q_ref[0] = (q * -(SB_HEAD_DIM ** -0.5)).astype(BF16)
    kf_ref[0] = k
    vf_ref[0] = v
    kb_ref[0] = k.astype(BF16)
    vb_ref[0] = v.astype(BF16)

    @pl.when(t == 0)
    def _():
        ext_ref[0:HIST_ROWS, :] = jnp.zeros((HIST_ROWS, POOL_WIDTH), F32)

    ext_ref[HIST_ROWS:, :] = pool_in
    sums, inv_cnt = [], []
    for g, w in enumerate(POOL_WINDOWS):
        cols = slice(g * POOL_GROUP, (g + 1) * POOL_GROUP)
        s = pool_in[:, cols]
        for i in range(1, w):
            s = s + ext_ref[HIST_ROWS - i:HIST_ROWS - i + tm, cols]
        sums.append(s)
        inv_cnt.append(1.0 / jnp.clip(pos + 1, 1, w).astype(F32))
    pool_ref[0] = _pool_mix(sums, pool_in, inv_cnt, pw_ref, ps_ref).astype(BF16)
    tail = ext_ref[tm:tm + HIST_ROWS, :]
    ext_ref[0:HIST_ROWS, :] = tail
    hist_ref[0] = tail


def _even_in(x3, g, w_in, pool_w, pool_scale):
    b = x3.shape[0]
    d_in = w_in.shape[1]
    tok = lambda width: pl.BlockSpec((1, TM, width), lambda bi, t: (bi, t, 0))
    sds = lambda width, dt: jax.ShapeDtypeStruct((b, TP, width), dt)
    return pl.pallas_call(
        _even_in_kernel,
        grid=(b, TP // TM),
        in_specs=[tok(D_MODEL), _const_spec((1, D_MODEL)), _const_spec((D_MODEL, d_in)),
                  _const_spec(pool_w.shape), _const_spec((1, POOL_WIDTH))],
        out_specs=[tok(POOL_WIDTH), tok(SB_WIDTH), tok(SB_WIDTH), tok(SB_WIDTH), tok(SB_WIDTH), tok(SB_WIDTH),
                   pl.BlockSpec((1, HIST_ROWS, POOL_WIDTH), lambda bi, t: (bi, 0, 0))],
        out_shape=[sds(POOL_WIDTH, BF16), sds(SB_WIDTH, BF16), sds(SB_WIDTH, BF16), sds(SB_WIDTH, BF16),
                   sds(SB_WIDTH, F32), sds(SB_WIDTH, F32),
                   jax.ShapeDtypeStruct((b, HIST_ROWS, POOL_WIDTH), F32)],
        scratch_shapes=[pltpu.VMEM((TM + HIST_ROWS, POOL_WIDTH), F32)],
        compiler_params=_params(2),
        name="even_in_prompt",
    )(x3, g, w_in, pool_w, pool_scale)


def _even_in_dec_kernel(x_ref, g_ref, w_ref, pw_ref, ps_ref, hist_ref,
                        pool_ref, q_ref, k_ref, v_ref, pin_ref):
    h = _rms(x_ref[...], g_ref[...]).astype(BF16)
    pool_in = _dot(h, w_ref[:, 0:POOL_WIDTH])
    q = _dot(h, w_ref[:, POOL_WIDTH:POOL_WIDTH + SB_WIDTH])
    k_ref[...] = _dot(h, w_ref[:, POOL_WIDTH + SB_WIDTH:POOL_WIDTH + 2 * SB_WIDTH])
    v_ref[...] = _dot(h, w_ref[:, POOL_WIDTH + 2 * SB_WIDTH:POOL_WIDTH + 3 * SB_WIDTH])
    q_ref[...] = (q * (SB_HEAD_DIM ** -0.5)).astype(BF16)
    pin_ref[...] = pool_in
    sums, inv_cnt = [], []
    for g, w in enumerate(POOL_WINDOWS):
        s = pool_in[:, g * POOL_GROUP:(g + 1) * POOL_GROUP]
        for i in range(1, w):
            base = (POOL_HIST - i) * POOL_WIDTH + g * POOL_GROUP
            s = s + hist_ref[:, base:base + POOL_GROUP]
        sums.append(s)
        inv_cnt.append(1.0 / w)
    pool_ref[...] = _pool_mix(sums, pool_in, inv_cnt, pw_ref, ps_ref).astype(BF16)


def _even_in_dec(x, g, w_in, pool_w, pool_scale, hist2d):
    n = x.shape[0]
    d_in = w_in.shape[1]
    full = lambda shape: pl.BlockSpec(shape, lambda i: (0,) * len(shape))
    return pl.pallas_call(
        _even_in_dec_kernel,
        grid=(1,),
        in_specs=[full((n, D_MODEL)), full((1, D_MODEL)), full((D_MODEL, d_in)), full(pool_w.shape),
                  full((1, POOL_WIDTH)), full(hist2d.shape)],
        out_specs=[full((n, POOL_WIDTH)), full((n, SB_WIDTH)), full((n, SB_WIDTH)), full((n, SB_WIDTH)),
                   full((n, POOL_WIDTH))],
        out_shape=[jax.ShapeDtypeStruct((n, POOL_WIDTH), BF16), jax.ShapeDtypeStruct((n, SB_WIDTH), BF16),
                   jax.ShapeDtypeStruct((n, SB_WIDTH), F32), jax.ShapeDtypeStruct((n, SB_WIDTH), F32),
                   jax.ShapeDtypeStruct((n, POOL_WIDTH), F32)],
        compiler_params=_params(1),
        name="even_in_decode",
    )(x, g, w_in, pool_w, pool_scale, hist2d)


def _sb_prompt_kernel(bias_ref, q_ref, k_ref, v_ref, o_ref, acc_ref, carry_ref):
    g = pl.program_id(1)
    acc_ref[...] = jnp.zeros_like(acc_ref)
    carry_ref[...] = jnp.zeros_like(carry_ref)
    r2 = lax.broadcasted_iota(jnp.int32, (2 * BLK, 2 * BLK), 0)
    c2 = lax.broadcasted_iota(jnp.int32, (2 * BLK, 2 * BLK), 1)
    same_head = (r2 < BLK) == (c2 < BLK)
    tri2 = jnp.where(same_head & (r2 >= c2), 1.0, 0.0).astype(BF16)
    ones2 = jnp.where(same_head, 1.0, 0.0).astype(BF16)
    lane = lax.broadcasted_iota(jnp.int32, (1, BLK), 1)
    first = lane < SB_HEAD_DIM
    lane2 = lax.broadcasted_iota(jnp.int32, (1, 2 * BLK), 1)
    key_in_blk = lane2 & (BLK - 1)
    q_row = g * Q_GROUP + lax.broadcasted_iota(jnp.int32, (Q_GROUP, 1), 0)
    neg_bias = [jnp.where(lane2 < BLK, -bias_ref[2 * pr], -bias_ref[2 * pr + 1]) for pr in range(HEAD_PAIRS)]

    def split_heads(ref, j, cols):
        x = ref[0, pl.ds(pl.multiple_of(j * BLK, BLK), BLK), cols].astype(F32)
        return jnp.concatenate([jnp.where(first, x, 0.0), jnp.where(first, 0.0, x)], axis=0).astype(BF16)

    def pair_block(pr, j, masked):
        cols = slice(pr * BLK, (pr + 1) * BLK)
        ccols = slice(2 * pr * BLK, 2 * (pr + 1) * BLK)
        nb_j = neg_bias[pr] + jnp.where(j * BLK + key_in_blk < PAD, PAD_LOGIT, 0.0)
        u = _dot_nt(q_ref[0, :, cols], split_heads(k_ref, j, cols)) + nb_j
        lm = jnp.minimum(u, 0.0) - jnp.log(1.0 + jnp.exp(-jnp.abs(u)))
        if masked:
            valid = (j * BLK + key_in_blk) < q_row
            lm = jnp.where(valid, lm, 0.0)
        lmb = lm.astype(BF16)
        carry = carry_ref[:, ccols]
        suffix = _dot(lmb, tri2) + carry
        carry_ref[:, ccols] = _dot(lmb, ones2) + carry
        w = jnp.exp(suffix - u)
        if masked:
            w = jnp.where(valid, w, 0.0)
        acc_ref[:, cols] += _dot(w.astype(BF16), split_heads(v_ref, j, cols))

    def key_block(j, masked):
        for pr in range(HEAD_PAIRS):
            pair_block(pr, j, masked)

    j_lo = g * BLOCKS_PER_GROUP

    def own_body(t, _):
        key_block(j_lo + BLOCKS_PER_GROUP - 1 - t, True)
        return 0

    def older_body(t, _):
        key_block(j_lo - 1 - t, False)
        return 0

    lax.fori_loop(0, BLOCKS_PER_GROUP, own_body, 0)
    lax.fori_loop(0, j_lo, older_body, 0)
    o_ref[0] = acc_ref[...].astype(BF16)


def _sb_prompt(q, k, v, bias):
    b = q.shape[0]
    q_spec = pl.BlockSpec((1, Q_GROUP, SB_WIDTH), lambda bi, g: (bi, g, 0))
    kv_spec = pl.BlockSpec((1, TP, SB_WIDTH), lambda bi, g: (bi, 0, 0), pipeline_mode=pl.Buffered(1))
    return pl.pallas_call(
        _sb_prompt_kernel,
        grid=(b, TP // Q_GROUP),
        in_specs=[pl.BlockSpec(memory_space=pltpu.SMEM), q_spec, kv_spec, kv_spec],
        out_specs=q_spec,
        out_shape=jax.ShapeDtypeStruct((b, TP, SB_WIDTH), BF16),
        scratch_shapes=[pltpu.VMEM((Q_GROUP, SB_WIDTH), F32), pltpu.VMEM((Q_GROUP, 2 * SB_WIDTH), F32)],
        compiler_params=_params(2),
        name="sb_prompt",
    )(bias, q, k, v)


PAGES_PER_STEP = 8


def _sb_decode_kernel(pt_ref, q_ref, bias_ref, *refs):
    k_refs = refs[:PAGES_PER_STEP]
    v_refs = refs[PAGES_PER_STEP:2 * PAGES_PER_STEP]
    o_ref, acc_ref, carry_ref = refs[2 * PAGES_PER_STEP:]
    i = pl.program_id(1)

    @pl.when(i == 0)
    def _():
        acc_ref[...] = jnp.zeros_like(acc_ref)
        carry_ref[...] = jnp.zeros_like(carry_ref)

    head = lax.broadcasted_iota(jnp.int32, (SB_HEADS, SB_WIDTH), 0)
    col = lax.broadcasted_iota(jnp.int32, (SB_HEADS, SB_WIDTH), 1)
    own = (col // SB_HEAD_DIM) == head
    qbd = jnp.where(own, jnp.broadcast_to(q_ref[0].astype(F32), (SB_HEADS, SB_WIDTH)), 0.0).astype(BF16)
    m2s = _cumsum_matrix()
    bias = bias_ref[...]
    zs = [_dot(qbd, k_refs[r][...].reshape(SB_WIDTH, PAGE_SIZE).astype(BF16)) + bias
          for r in range(PAGES_PER_STEP)]
    stacked = jnp.concatenate([_neg_softplus(z) for z in zs], axis=0)
    hi = stacked.astype(BF16)
    lo = (stacked - hi.astype(F32)).astype(BF16)
    res = _dot(jnp.concatenate([hi, lo], axis=1), m2s)
    offset = carry_ref[...]
    acc = acc_ref[...]
    for r in range(PAGES_PER_STEP):
        rows = slice(r * SB_HEADS, (r + 1) * SB_HEADS)
        w = jnp.exp(zs[r] + res[rows, :BLK] + offset)
        offset = offset + res[rows, BLK:]
        vt = v_refs[r][...].reshape(SB_WIDTH, PAGE_SIZE).astype(BF16)
        acc = acc + _dot_nt(w.astype(BF16), vt)
    carry_ref[...] = offset
    acc_ref[...] = acc

    @pl.when(i == pl.num_programs(1) - 1)
    def _():
        o_ref[0] = jnp.sum(jnp.where(own, acc, 0.0), axis=0, keepdims=True).astype(BF16)


def _sb_decode(page_table, q3, bias_b, cache_kt, cache_vt, e):
    nb, n_pages = page_table.shape
    steps = n_pages // PAGES_PER_STEP

    def page_spec(r):
        return pl.BlockSpec((None, None, SB_HEADS, SB_HEAD_DIM, PAGE_SIZE),
                            lambda b, i, pt: (e, pt[b, n_pages - 1 - (i * PAGES_PER_STEP + r)], 0, 0, 0))

    grid_spec = pltpu.PrefetchScalarGridSpec(
        num_scalar_prefetch=1,
        grid=(nb, steps),
        in_specs=[pl.BlockSpec((1, 1, SB_WIDTH), lambda b, i, pt: (b, 0, 0)),
                  pl.BlockSpec((SB_HEADS, BLK), lambda b, i, pt: (0, 0))]
                 + [page_spec(r) for r in range(PAGES_PER_STEP)] * 2,
        out_specs=pl.BlockSpec((1, 1, SB_WIDTH), lambda b, i, pt: (b, 0, 0)),
        scratch_shapes=[pltpu.VMEM((SB_HEADS, SB_WIDTH), F32), pltpu.VMEM((SB_HEADS, BLK), F32)],
    )
    return pl.pallas_call(
        _sb_decode_kernel,
        grid_spec=grid_spec,
        out_shape=jax.ShapeDtypeStruct((nb, 1, SB_WIDTH), BF16),
        compiler_params=_params(2),
        name="sb_decode",
    )(page_table, q3, bias_b, *([cache_kt] * PAGES_PER_STEP), *([cache_vt] * PAGES_PER_STEP))


def _odd_in_kernel(pad, x_ref, g_ref, w_ref, cos_ref, sin_ref, q_ref, k_ref, v_ref, sg_ref):
    t = pl.program_id(1)
    tm = x_ref.shape[1]
    h = _rms(x_ref[0], g_ref[...]).astype(BF16)
    dq = RET_HEADS * RET_QK_DIM
    dv = RET_HEADS * RET_V_DIM
    cos = cos_ref[...]
    sin = sin_ref[...]
    half = RET_QK_DIM // 2
    if pad:
        real = (t * tm + lax.broadcasted_iota(jnp.int32, (tm, 1), 0)) >= pad
    for hd in range(RET_HEADS):
        for which, out_ref, scale in ((0, q_ref, 1.0), (1, k_ref, RET_QK_DIM ** -0.5)):
            base = which * dq + hd * RET_QK_DIM
            a = _dot(h, w_ref[:, base:base + RET_QK_DIM])
            x1, x2 = a[:, :half], a[:, half:]
            rot = jnp.concatenate([x1 * cos - x2 * sin, x1 * sin + x2 * cos], axis=1) * scale
            if pad and which == 1:
                rot = jnp.where(real, rot, 0.0)
            out_ref[0, :, hd * RET_QK_DIM:(hd + 1) * RET_QK_DIM] = rot.astype(BF16)
        cols = slice(hd * RET_V_DIM, (hd + 1) * RET_V_DIM)
        v = _dot(h, w_ref[:, 2 * dq + hd * RET_V_DIM:2 * dq + (hd + 1) * RET_V_DIM])
        if pad:
            v = jnp.where(real, v, 0.0)
        v_ref[0, :, cols] = v.astype(BF16)
        gt = _dot(h, w_ref[:, 2 * dq + dv + hd * RET_V_DIM:2 * dq + dv + (hd + 1) * RET_V_DIM])
        sg_ref[0, :, cols] = gt * jax.nn.sigmoid(gt)


def _odd_in(x3, g, w_in, cos, sin, tm, pad):
    b, t_len, _ = x3.shape
    dq = RET_HEADS * RET_QK_DIM
    dv = RET_HEADS * RET_V_DIM
    tok = lambda width: pl.BlockSpec((1, tm, width), lambda bi, t: (bi, t, 0))
    tab = pl.BlockSpec((tm, RET_QK_DIM // 2), lambda bi, t: (t, 0))
    sds = lambda width, dt: jax.ShapeDtypeStruct((b, t_len, width), dt)
    return pl.pallas_call(
        functools.partial(_odd_in_kernel, pad),
        grid=(b, t_len // tm),
        in_specs=[tok(D_MODEL), _const_spec((1, D_MODEL)), _const_spec(w_in.shape), tab, tab],
        out_specs=[tok(dq), tok(dq), tok(dv), tok(dv)],
        out_shape=[sds(dq, BF16), sds(dq, BF16), sds(dv, BF16), sds(dv, F32)],
        compiler_params=_params(2),
        name="odd_in",
    )(x3, g, w_in, cos, sin)


def _ret_prompt_kernel(lg_ref, q_ref, k_ref, v_ref, sg_ref, o_ref, st_ref, state_ref):
    c = pl.program_id(1)

    @pl.when(c == 0)
    def _():
        state_ref[...] = jnp.zeros_like(state_ref)

    row = lax.broadcasted_iota(jnp.int32, (BLK, BLK), 0)
    colm = lax.broadcasted_iota(jnp.int32, (BLK, BLK), 1)
    diff = (row - colm).astype(F32)
    idx = lax.broadcasted_iota(jnp.int32, (BLK, 1), 0).astype(F32)
    for hd in range(RET_HEADS):
        lg = lg_ref[hd]
        decay = jnp.where(diff >= 0, jnp.exp(lg * jnp.maximum(diff, 0.0)), 0.0)
        q_dec = jnp.exp(lg * (idx + 1.0))
        k_dec = jnp.exp(lg * (BLK - 1.0 - idx))
        chunk_dec = jnp.exp(jnp.full((1, 1), BLK, F32) * lg)
        qk_cols = slice(hd * RET_QK_DIM, (hd + 1) * RET_QK_DIM)
        v_cols = slice(hd * RET_V_DIM, (hd + 1) * RET_V_DIM)
        q = q_ref[0, :, qk_cols]
        k = k_ref[0, :, qk_cols]
        v = v_ref[0, :, v_cols]
        state = state_ref[hd]
        scores = (_dot_nt(q, k) * decay).astype(BF16)
        o = _dot(scores, v) + _dot(q, state.astype(BF16)) * q_dec
        new_state = state * chunk_dec + _dot_tn((k.astype(F32) * k_dec).astype(BF16), v)
        state_ref[hd] = new_state
        on = o * lax.rsqrt(jnp.mean(o * o, axis=-1, keepdims=True) + EPS)
        o_ref[0, :, v_cols] = (sg_ref[0, :, v_cols] * on).astype(BF16)

    @pl.when(c == pl.num_programs(1) - 1)
    def _():
        st_ref[0] = state_ref[...]


def _ret_prompt(lg, q, k, v, sg):
    b = q.shape[0]
    qk_spec = pl.BlockSpec((1, BLK, RET_HEADS * RET_QK_DIM), lambda bi, c: (bi, c, 0))
    v_spec = pl.BlockSpec((1, BLK, RET_HEADS * RET_V_DIM), lambda bi, c: (bi, c, 0))
    st_shape = (RET_HEADS, RET_QK_DIM, RET_V_DIM)
    return pl.pallas_call(
        _ret_prompt_kernel,
        grid=(b, NB),
        in_specs=[pl.BlockSpec(memory_space=pltpu.SMEM), qk_spec, qk_spec, v_spec, v_spec],
        out_specs=[v_spec, pl.BlockSpec((1,) + st_shape, lambda bi, c: (bi, 0, 0, 0))],
        out_shape=[jax.ShapeDtypeStruct((b, TP, RET_HEADS * RET_V_DIM), BF16),
                   jax.ShapeDtypeStruct((b,) + st_shape, F32)],
        scratch_shapes=[pltpu.VMEM(st_shape, F32)],
        compiler_params=_params(2),
        name="retention_prompt",
    )(lg, q, k, v, sg)


def _ret_decode_kernel(lg_ref, q_ref, k_ref, v_ref, sg_ref, s_ref, o_ref, st_ref):
    row0 = lax.broadcasted_iota(jnp.int32, (8, 1), 0) == 0
    for hd in range(RET_HEADS):
        lg = lg_ref[hd]
        gamma = jnp.exp(jnp.full((1, 1), 1.0, F32) * lg)
        qk_cols = slice(hd * RET_QK_DIM, (hd + 1) * RET_QK_DIM)
        v_cols = slice(hd * RET_V_DIM, (hd + 1) * RET_V_DIM)
        qf = q_ref[0, :, qk_cols].astype(F32)
        kf = k_ref[0, :, qk_cols].astype(F32)
        vf = v_ref[0, :, v_cols].astype(F32)
        state = s_ref[0, hd]
        q8 = jnp.where(row0, jnp.broadcast_to(qf, (8, RET_QK_DIM)), 0.0).astype(BF16)
        k8 = jnp.where(row0, jnp.broadcast_to(kf, (8, RET_QK_DIM)), 0.0).astype(BF16)
        v8 = jnp.broadcast_to(vf, (8, RET_V_DIM)).astype(BF16)
        score = jnp.sum(qf * kf, axis=-1, keepdims=True).astype(BF16).astype(F32)
        cross = _dot(q8, state.astype(BF16))[0:1, :] * gamma
        o = score * vf + cross
        st_ref[0, hd] = state * gamma + _dot_tn(k8, v8)
        on = o * lax.rsqrt(jnp.mean(o * o, axis=-1, keepdims=True) + EPS)
        o_ref[0, :, v_cols] = (sg_ref[0, :, v_cols] * on).astype(BF16)


def _ret_decode(lg, q, k, v, sg, state):
    nb = q.shape[0]
    qk_spec = pl.BlockSpec((1, 1, RET_HEADS * RET_QK_DIM), lambda bi: (bi, 0, 0))
    v_spec = pl.BlockSpec((1, 1, RET_HEADS * RET_V_DIM), lambda bi: (bi, 0, 0))
    st_spec = pl.BlockSpec((1, RET_HEADS, RET_QK_DIM, RET_V_DIM), lambda bi: (bi, 0, 0, 0))
    return pl.pallas_call(
        _ret_decode_kernel,
        grid=(nb,),
        in_specs=[pl.BlockSpec(memory_space=pltpu.SMEM), qk_spec, qk_spec, v_spec, v_spec, st_spec],
        out_specs=[v_spec, st_spec],
        out_shape=[jax.ShapeDtypeStruct((nb, 1, RET_HEADS * RET_V_DIM), BF16),
                   jax.ShapeDtypeStruct(state.shape, F32)],
        compiler_params=_params(1),
        name="retention_decode",
    )(lg, q, k, v, sg, state)


def _rope_tables(pos):
    half = RET_QK_DIM // 2
    inv = ROPE_BASE ** (-jnp.arange(half, dtype=F32) / half)
    ang = pos.astype(F32)[:, None] * inv[None, :]
    return jnp.cos(ang), jnp.sin(ang)


def kernel(x_prompt, x_sample, cache_k, cache_v, state_pool, state_ret, page_table, meta_tokens, norm_g,
           ffn_w_gate, ffn_w_up, ffn_w_down, even_w_in, even_w_out, pool_w, pool_scale, sb_bias,
           odd_w_in, odd_w_out):
    b = x_prompt.shape[0]
    db = x_sample.shape[0]
    depth = norm_g.shape[0]

    xp = jnp.concatenate([jnp.zeros((b, PAD, D_MODEL), F32),
                          jnp.broadcast_to(meta_tokens[None].astype(F32), (b, N_META, D_MODEL)),
                          x_prompt], axis=1).reshape(b * TP, D_MODEL)
    xs = x_sample.reshape(db, D_MODEL)

    lg = jnp.log1p(-jnp.exp2(-5.0 - jnp.arange(RET_HEADS, dtype=F32)))
    cos_p, sin_p = _rope_tables(jnp.arange(TP) - PAD)
    cos_s, sin_s = _rope_tables(jnp.full((db,), PAST_LEN))

    cache_kt = jnp.transpose(cache_k, (0, 1, 3, 4, 2))
    cache_vt = jnp.transpose(cache_v, (0, 1, 3, 4, 2))

    wg = ffn_w_gate.astype(BF16)
    wu = ffn_w_up.astype(BF16)
    wd = ffn_w_down.astype(BF16)

    def ffn(x, layer, half, tm, **mix):
        g2 = norm_g[layer, 4 * half:4 * half + 2]
        return _ffn(x, g2, wg[layer, half], wu[layer, half], wd[layer, half], tm, **mix)

    pool_p, pool_s, k_p, v_p, k_s, v_s, ret_p, ret_s = [], [], [], [], [], [], [], []
    for layer in range(depth):
        xp = ffn(xp, layer, 0, TM)
        xs = ffn(xs, layer, 0, db)
        g_in = norm_g[layer, 2:3]
        g_out = norm_g[layer, 3:4]
        if layer % 2 == 0:
            e = layer // 2
            w_in = even_w_in[e].astype(BF16)
            w_out = even_w_out[e].astype(BF16)
            pw = pool_w[e].astype(BF16)
            ps = pool_scale[e][None]
            pool_o, q, kb, vb, kf, vf, hist = _even_in(xp.reshape(b, TP, D_MODEL), g_in, w_in, pw, ps)
            sb_o = _sb_prompt(q, kb, vb, sb_bias[e])
            ws_out = [w_out[:POOL_WIDTH], w_out[POOL_WIDTH:]]
            mix_p = [pool_o.reshape(b * TP, POOL_WIDTH), sb_o.reshape(b * TP, SB_WIDTH)]
            pool_p.append(hist[:, HIST_ROWS - POOL_HIST:])
            k_p.append(kf[:, PAD:].reshape(b, T_REAL, SB_HEADS, SB_HEAD_DIM))
            v_p.append(vf[:, PAD:].reshape(b, T_REAL, SB_HEADS, SB_HEAD_DIM))

            hist_s = state_pool[e]
            pool_os, qs, ks, vs, pin_s = _even_in_dec(xs, g_in, w_in, pw, ps,
                                                     hist_s.reshape(db, POOL_HIST * POOL_WIDTH))
            sb_os = _sb_decode(page_table, qs.reshape(db, 1, SB_WIDTH),
                               jnp.broadcast_to(sb_bias[e][:, None], (SB_HEADS, BLK)),
                               cache_kt, cache_vt, e)
            mix_s = [pool_os, sb_os.reshape(db, SB_WIDTH)]
            pool_s.append(jnp.concatenate([hist_s[:, 1:], pin_s[:, None]], axis=1))
            k_s.append(ks.reshape(db, 1, SB_HEADS, SB_HEAD_DIM))
            v_s.append(vs.reshape(db, 1, SB_HEADS, SB_HEAD_DIM))
        else:
            o = layer // 2
            w_in = odd_w_in[o].astype(BF16)
            w_out = odd_w_out[o].astype(BF16)
            q, k, v, sg = _odd_in(xp.reshape(b, TP, D_MODEL), g_in, w_in, cos_p, sin_p, TM, PAD)
            gated, st = _ret_prompt(lg, q, k, v, sg)
            ws_out = [w_out]
            mix_p = [gated.reshape(b * TP, RET_HEADS * RET_V_DIM)]
            ret_p.append(st)

            qs, ks, vs, sgs = _odd_in(xs.reshape(1, db, D_MODEL), g_in, w_in, cos_s, sin_s, db, 0)
            to_rows = lambda a: a.reshape(db, 1, a.shape[-1])
            gated_s, st_s = _ret_decode(lg, to_rows(qs), to_rows(ks), to_rows(vs), to_rows(sgs), state_ret[o])
            mix_s = [gated_s.reshape(db, RET_HEADS * RET_V_DIM)]
            ret_s.append(st_s)
        xp = ffn(xp, layer, 1, TM, g_mix=g_out, parts=mix_p, ws=ws_out)
        xs = ffn(xs, layer, 1, db, g_mix=g_out, parts=mix_s, ws=ws_out)

    y_prompt = xp.reshape(b, TP, D_MODEL)[:, TP - SEQ:]
    y_sample = xs.reshape(db, 1, D_MODEL)
    return (y_prompt, y_sample, jnp.stack(pool_p), jnp.stack(pool_s), jnp.stack(k_p), jnp.stack(v_p),
            jnp.stack(k_s), jnp.stack(v_s), jnp.stack(ret_p), jnp.stack(ret_s))
```
